```python
import jax, jax.numpy as jnp
from jax import lax
import numpy as np

D_MODEL = 2048
BATCH = 2
SEQ = 8192
DEPTH = 2

MEM_LEN = 256
N_BRANCH = 4
BRANCH_W = 1024
POOL_GROUPS = 4
POOL_WINDOWS = (2, 4, 8, 16)
POOL_GW = BRANCH_W // POOL_GROUPS
MLA_HEADS = 8
Q_LORA = 512
KV_LORA = 512
QK_NOPE = 128
QK_ROPE = 64
QK_HEAD = QK_NOPE + QK_ROPE
V_HEAD = 128
ROPE_THETA = 10000.0
CONV_W = 3
XATTN_HEADS = 4
XATTN_HEAD_DIM = BRANCH_W // XATTN_HEADS
Q_BLOCK = 128
EPS = 1e-6
IN_SPLITS = (BRANCH_W, BRANCH_W, Q_LORA, KV_LORA, QK_ROPE, BRANCH_W, BRANCH_W, BRANCH_W, BRANCH_W, BRANCH_W, BRANCH_W, BRANCH_W, N_BRANCH * D_MODEL)
N_IN = 9 * BRANCH_W + Q_LORA + KV_LORA + QK_ROPE + N_BRANCH * D_MODEL

kernel_name = 'hybrid_gated_pool_mla_conv_memxattn'


def rms_norm(x, g):
    xf = x.astype(jnp.float32)
    y = xf * lax.rsqrt(jnp.mean(xf * xf, axis=-1, keepdims=True) + EPS)
    return (y * g.astype(jnp.float32)).astype(x.dtype)


def rope_tables(positions):
    inv = ROPE_THETA ** (-jnp.arange(0, QK_ROPE, 2, dtype=jnp.float32) / QK_ROPE)
    ang = positions.astype(jnp.float32)[..., None] * inv
    return jnp.cos(ang)[:, :, None, :], jnp.sin(ang)[:, :, None, :]


def rotate_tail(xh, cos, sin):
    nope = xh[..., :QK_NOPE]
    r = xh[..., QK_NOPE:].astype(jnp.float32)
    r1, r2 = r[..., :QK_ROPE // 2], r[..., QK_ROPE // 2:]
    rot = jnp.concatenate([r1 * cos - r2 * sin, r2 * cos + r1 * sin], axis=-1).astype(xh.dtype)
    return jnp.concatenate([nope, rot], axis=-1)


def pool_mixer(v, pool_w, pool_scale):
    B, S, _ = v.shape
    vg = v.reshape(B, S, POOL_GROUPS, POOL_GW).astype(jnp.float32)
    cs = jnp.cumsum(vg, axis=1)
    win = jnp.array(POOL_WINDOWS, dtype=jnp.int32)
    t = jnp.arange(S, dtype=jnp.int32)
    prev = t[:, None] - win[None, :]
    cs_prev = cs[:, jnp.maximum(prev, 0), jnp.arange(POOL_GROUPS)[None, :], :]
    cs_prev = jnp.where((prev >= 0)[None, :, :, None], cs_prev, 0.0)
    cnt = jnp.minimum(t[:, None] + 1, win[None, :]).astype(jnp.float32)
    mixed = ((cs - cs_prev) / cnt[None, :, :, None] - vg).astype(v.dtype)
    out = jnp.einsum('bsgc,gcd->bsgd', mixed, pool_w)
    return out.reshape(B, S, BRANCH_W) * pool_scale


def causal_block_attention(q, k, v):
    B, S, H, Dh = q.shape
    nb = S // Q_BLOCK
    scale = Dh ** -0.5
    qb = q.reshape(B, nb, Q_BLOCK, H, Dh).transpose(1, 0, 2, 3, 4)
    starts = jnp.arange(nb, dtype=jnp.int32) * Q_BLOCK
    kpos = jnp.arange(S, dtype=jnp.int32)

    def one_block(args):
        qi, s0 = args
        s = jnp.einsum('bqhd,bkhd->bhqk', qi, k).astype(jnp.float32) * scale
        mask = kpos[None, :] <= (s0 + jnp.arange(Q_BLOCK, dtype=jnp.int32))[:, None]
        s = jnp.where(mask[None, None], s, -jnp.inf)
        p = jax.nn.softmax(s, axis=-1).astype(v.dtype)
        return jnp.einsum('bhqk,bkhd->bqhd', p, v)

    o = lax.map(one_block, (qb, starts))
    return o.transpose(1, 0, 2, 3, 4).reshape(B, S, H, v.shape[-1])


def mla_mixer(cq, ckv, krope, cos, sin, q_a_g, kv_a_g, w_uq, w_ukv, q_g, k_g):
    B, S, _ = cq.shape
    q = (rms_norm(cq, q_a_g) @ w_uq).reshape(B, S, MLA_HEADS, QK_HEAD)
    kv = (rms_norm(ckv, kv_a_g) @ w_ukv).reshape(B, S, MLA_HEADS, QK_NOPE + V_HEAD)
    k_nope, v = kv[..., :QK_NOPE], kv[..., QK_NOPE:]
    k = jnp.concatenate([k_nope, jnp.broadcast_to(krope[:, :, None, :], (B, S, MLA_HEADS, QK_ROPE))], axis=-1)
    q = rotate_tail(rms_norm(q, q_g), cos, sin)
    k = rotate_tail(rms_norm(k, k_g), cos, sin)
    o = causal_block_attention(q, k, v)
    return o.reshape(B, S, MLA_HEADS * V_HEAD)


def conv_mixer(b, c, xc, conv_w):
    u = c * xc
    y = lax.conv_general_dilated(u, conv_w[:, None, :].astype(u.dtype), window_strides=(1,), padding=[(CONV_W - 1, 0)], dimension_numbers=('NWC', 'WIO', 'NWC'), feature_group_count=BRANCH_W)
    return b * y


def memory_xattn(q, mem_kv, q_g, k_g):
    B, S, _ = q.shape
    M = mem_kv.shape[1]
    qh = rms_norm(q.reshape(B, S, XATTN_HEADS, XATTN_HEAD_DIM), q_g)
    k = rms_norm(mem_kv[..., :BRANCH_W].reshape(B, M, XATTN_HEADS, XATTN_HEAD_DIM), k_g)
    v = mem_kv[..., BRANCH_W:].reshape(B, M, XATTN_HEADS, XATTN_HEAD_DIM)
    s = jnp.einsum('bshd,bmhd->bhsm', qh, k).astype(jnp.float32) * (XATTN_HEAD_DIM ** -0.5)
    p = jax.nn.softmax(s, axis=-1).astype(v.dtype)
    return jnp.einsum('bhsm,bmhd->bshd', p, v).reshape(B, S, BRANCH_W)


def setup_inputs(seed: int = 0) -> dict:
    key = jax.random.key(seed)
    ks = jax.random.split(key, 24)
    f32 = jnp.float32

    def nrm(k, shape, scale):
        return jax.random.normal(k, shape, f32) * scale

    def gain(k, shape):
        return 1.0 + 0.02 * jax.random.normal(k, shape, f32)

    offs = jax.random.randint(ks[2], (BATCH, 1), 0, 4096, dtype=jnp.int32)
    positions = offs + jnp.arange(SEQ, dtype=jnp.int32)[None, :]
    return {
        'x': nrm(ks[0], (BATCH, SEQ, D_MODEL), 1.0),
        'mem': nrm(ks[1], (BATCH, MEM_LEN, D_MODEL), 1.0),
        'positions': positions,
        'norm_g': gain(ks[3], (DEPTH, D_MODEL)),
        'w_in': nrm(ks[4], (DEPTH, D_MODEL, N_IN), D_MODEL ** -0.5),
        'gate_b': nrm(ks[5], (DEPTH, N_BRANCH * D_MODEL), 0.02),
        'pool_w': nrm(ks[6], (DEPTH, POOL_GROUPS, POOL_GW, POOL_GW), POOL_GW ** -0.5),
        'pool_scale': gain(ks[7], (DEPTH, BRANCH_W)),
        'q_a_norm_g': gain(ks[8], (DEPTH, Q_LORA)),
        'kv_a_norm_g': gain(ks[9], (DEPTH, KV_LORA)),
        'w_uq': nrm(ks[10], (DEPTH, Q_LORA, MLA_HEADS * QK_HEAD), Q_LORA ** -0.5),
        'w_ukv': nrm(ks[11], (DEPTH, KV_LORA, MLA_HEADS * (QK_NOPE + V_HEAD)), KV_LORA ** -0.5),
        'mla_q_norm_g': gain(ks[12], (DEPTH, QK_HEAD)),
        'mla_k_norm_g': gain(ks[13], (DEPTH, QK_HEAD)),
        'conv_w': nrm(ks[14], (DEPTH, CONV_W, BRANCH_W), CONV_W ** -0.5),
        'mem_norm_g': gain(ks[15], (DEPTH, D_MODEL)),
        'w_mem_kv': nrm(ks[16], (DEPTH, D_MODEL, 2 * BRANCH_W), D_MODEL ** -0.5),
        'xattn_q_norm_g': gain(ks[17], (DEPTH, XATTN_HEAD_DIM)),
        'xattn_k_norm_g': gain(ks[18], (DEPTH, XATTN_HEAD_DIM)),
        'w_branch': nrm(ks[19], (DEPTH, N_BRANCH, BRANCH_W, D_MODEL), BRANCH_W ** -0.5),
        'w_out': nrm(ks[20], (DEPTH, D_MODEL, D_MODEL), D_MODEL ** -0.5),
    }


def reference(x, mem, positions, norm_g, w_in, gate_b, pool_w, pool_scale, q_a_norm_g, kv_a_norm_g, w_uq, w_ukv, mla_q_norm_g, mla_k_norm_g, conv_w, mem_norm_g, w_mem_kv, xattn_q_norm_g, xattn_k_norm_g, w_branch, w_out):
    B, S, _ = x.shape
    cos, sin = rope_tables(positions)
    split_points = np.cumsum(IN_SPLITS)[:-1].tolist()
    for l in range(DEPTH):
        h = rms_norm(x, norm_g[l])
        proj = h @ w_in[l]
        (pv, pz, cq, ckv, kr, mz, cb, cc, cx, cz, xq, xz, gpre) = jnp.split(proj, split_points, axis=-1)
        y_pool = pool_mixer(pv, pool_w[l], pool_scale[l]) * jax.nn.silu(pz)
        y_mla = mla_mixer(cq, ckv, kr, cos, sin, q_a_norm_g[l], kv_a_norm_g[l], w_uq[l], w_ukv[l], mla_q_norm_g[l], mla_k_norm_g[l]) * jax.nn.silu(mz)
        y_conv = conv_mixer(cb, cc, cx, conv_w[l]) * jax.nn.silu(cz)
        mem_kv = rms_norm(mem, mem_norm_g[l]) @ w_mem_kv[l]
        y_mem = memory_xattn(xq, mem_kv, xattn_q_norm_g[l], xattn_k_norm_g[l]) * jax.nn.silu(xz)
        gates = jax.nn.sigmoid((gpre + gate_b[l]).astype(jnp.float32)).astype(x.dtype).reshape(B, S, N_BRANCH, D_MODEL)
        merged = gates[:, :, 0] * (y_pool @ w_branch[l, 0])
        merged = merged + gates[:, :, 1] * (y_mla @ w_branch[l, 1])
        merged = merged + gates[:, :, 2] * (y_conv @ w_branch[l, 2])
        merged = merged + gates[:, :, 3] * (y_mem @ w_branch[l, 3])
        x = x + merged @ w_out[l]
    return x
```

```python
import functools

import jax
import jax.numpy as jnp
from jax import lax
from jax.experimental import pallas as pl
from jax.experimental.pallas import tpu as pltpu

D_MODEL = 2048
MEM_LEN = 256
N_BRANCH = 4
BRANCH_W = 1024
POOL_GROUPS = 4
POOL_WINDOWS = (2, 4, 8, 16)
POOL_GW = BRANCH_W // POOL_GROUPS
MLA_HEADS = 8
Q_LORA = 512
KV_LORA = 512
QK_NOPE = 128
QK_ROPE = 64
QK_HEAD = QK_NOPE + QK_ROPE
V_HEAD = 128
ROPE_THETA = 10000.0
CONV_W = 3
XATTN_HEADS = 4
XATTN_HEAD_DIM = BRANCH_W // XATTN_HEADS
EPS = 1e-6

LANES = 128
HEAD_PAD = 2 * LANES
HALO = 16
LAT_W = Q_LORA + KV_LORA + LANES
VMEM_LIMIT = 56 * 1024 * 1024

(COL_PV, COL_CB, COL_CC, COL_CX, COL_XQ, COL_PZ, COL_MZ, COL_CZ, COL_XZ, COL_GATE) = range(10)
N_PLAIN, N_SILU = 5, 4
N_MAIN = (N_PLAIN + N_SILU) * BRANCH_W + N_BRANCH * D_MODEL

BF16 = jnp.bfloat16
F32 = jnp.float32


def _params(*sem):
    return pltpu.CompilerParams(dimension_semantics=sem, vmem_limit_bytes=VMEM_LIMIT)


def _sigmoid(v):
    return 1.0 / (1.0 + jnp.exp(-v))


def _rms(v, n):
    return lax.rsqrt(jnp.sum(v * v, axis=-1, keepdims=True) * (1.0 / n) + EPS)


def _dot(a, b):
    return jnp.dot(a, b, preferred_element_type=F32)


def _dot_nt(a, b):
    return lax.dot_general(a, b, (((1,), (1,)), ((), ())), preferred_element_type=F32)


def _rope_table_kernel(pos_ref, inv_ref, c_ref, s1_ref, s2_ref):
    ang = pos_ref[...].astype(F32) * inv_ref[...]
    lane = lax.broadcasted_iota(jnp.int32, ang.shape, 1)
    cos, sin = jnp.cos(ang), jnp.sin(ang)
    c_ref[...] = jnp.where(lane < QK_ROPE, cos, 0.0)
    s1_ref[...] = jnp.where((lane >= QK_ROPE // 2) & (lane < QK_ROPE), sin, 0.0)
    s2_ref[...] = jnp.where(lane < QK_ROPE // 2, -sin, 0.0)


def _rope_tables(positions, tm):
    m = positions.size
    inv = ROPE_THETA ** (-jnp.arange(0, QK_ROPE, 2, dtype=F32) / QK_ROPE)
    inv_tile = jnp.concatenate([inv, inv, jnp.zeros((LANES - QK_ROPE,), F32)])[None, :]
    tab = jax.ShapeDtypeStruct((m, LANES), F32)
    return pl.pallas_call(
        _rope_table_kernel,
        grid=(m // tm,),
        in_specs=[pl.BlockSpec((tm, 1), lambda i: (i, 0)), pl.BlockSpec((1, LANES), lambda i: (0, 0))],
        out_specs=[pl.BlockSpec((tm, LANES), lambda i: (i, 0))] * 3,
        out_shape=[tab, tab, tab],
        compiler_params=_params("parallel"),
        name="rope_tables",
    )(positions.reshape(m, 1), inv_tile)


def _in_proj_kernel(x_ref, g_ref, w_ref, b_ref, o_ref, h_ref):
    j = pl.program_id(1)

    @pl.when(j == 0)
    def _():
        x = x_ref[...]
        h_ref[...] = (x * _rms(x, D_MODEL) * g_ref[...]).astype(BF16)

    acc = _dot(h_ref[...], w_ref[...])

    @pl.when(j < N_PLAIN)
    def _():
        o_ref[...] = acc.astype(BF16)

    @pl.when((j >= N_PLAIN) & (j < N_PLAIN + N_SILU))
    def _():
        o_ref[...] = (acc * _sigmoid(acc)).astype(BF16)

    @pl.when(j >= N_PLAIN + N_SILU)
    def _():
        o_ref[...] = _sigmoid(acc + b_ref[...]).astype(BF16)


def _in_proj(x2, norm_g, w_main, b_main, tm, tn):
    m = x2.shape[0]
    return pl.pallas_call(
        _in_proj_kernel,
        grid=(m // tm, N_MAIN // tn),
        in_specs=[
            pl.BlockSpec((tm, D_MODEL), lambda i, j: (i, 0)),
            pl.BlockSpec((1, D_MODEL), lambda i, j: (0, 0)),
            pl.BlockSpec((D_MODEL, tn), lambda i, j: (0, j)),
            pl.BlockSpec((1, tn), lambda i, j: (0, j)),
        ],
        out_specs=pl.BlockSpec((tm, tn), lambda i, j: (i, j)),
        out_shape=jax.ShapeDtypeStruct((m, N_MAIN), BF16),
        scratch_shapes=[pltpu.VMEM((tm, D_MODEL), BF16)],
        compiler_params=_params("parallel", "arbitrary"),
        name="in_proj",
    )(x2, norm_g, w_main, b_main)


def _rope_tile(t, c, s1, s2):
    return t * c + pltpu.roll(t, QK_ROPE // 2, 1) * s1 + pltpu.roll(t, LANES - QK_ROPE // 2, 1) * s2


def _mla_prep_kernel(x_ref, g_ref, wlat_ref, qag_ref, kvag_ref, wuq_ref, wukv_ref, qg_ref, kgn_ref, kgt_ref,
                     c_ref, s1_ref, s2_ref, q_out, k_out, v_out):
    x = x_ref[...]
    h = (x * _rms(x, D_MODEL) * g_ref[...]).astype(BF16)
    lat = _dot(h, wlat_ref[...])
    cq = lat[:, :Q_LORA]
    ckv = lat[:, Q_LORA:Q_LORA + KV_LORA]
    kr = lat[:, Q_LORA + KV_LORA:]
    q = _dot((cq * _rms(cq, Q_LORA) * qag_ref[...]).astype(BF16), wuq_ref[...])
    kv = _dot((ckv * _rms(ckv, KV_LORA) * kvag_ref[...]).astype(BF16), wukv_ref[...])
    c, s1, s2 = c_ref[...], s1_ref[...], s2_ref[...]
    kr_ss = jnp.sum(kr * kr, axis=-1, keepdims=True)
    kr_rot = _rope_tile(kr * kgt_ref[...], c, s1, s2)
    qg = qg_ref[...]
    scale = QK_HEAD ** -0.5
    for hd in range(MLA_HEADS):
        qh = q[:, hd * HEAD_PAD:(hd + 1) * HEAD_PAD]
        qn = qh * (_rms(qh, QK_HEAD) * scale) * qg
        q_out[:, hd * HEAD_PAD:hd * HEAD_PAD + LANES] = qn[:, :LANES].astype(BF16)
        q_out[:, hd * HEAD_PAD + LANES:(hd + 1) * HEAD_PAD] = _rope_tile(qn[:, LANES:], c, s1, s2).astype(BF16)
        kn = kv[:, hd * QK_NOPE:(hd + 1) * QK_NOPE]
        rk = lax.rsqrt((jnp.sum(kn * kn, axis=-1, keepdims=True) + kr_ss) * (1.0 / QK_HEAD) + EPS)
        k_out[:, hd * HEAD_PAD:hd * HEAD_PAD + LANES] = (kn * rk * kgn_ref[...]).astype(BF16)
        k_out[:, hd * HEAD_PAD + LANES:(hd + 1) * HEAD_PAD] = (kr_rot * rk).astype(BF16)
    v_out[...] = kv[:, MLA_HEADS * QK_NOPE:].astype(BF16)


def _mla_prep(x2, norm_g, w_lat, qag, kvag, w_uq_p, w_ukv_p, qg_pad, kg_nope, kg_tail, tabs, tm):
    m = x2.shape[0]
    row = lambda w: pl.BlockSpec((tm, w), lambda i: (i, 0))
    full = lambda a: pl.BlockSpec(a.shape, lambda i: (0,) * a.ndim)
    consts = (norm_g, w_lat, qag, kvag, w_uq_p, w_ukv_p, qg_pad, kg_nope, kg_tail)
    return pl.pallas_call(
        _mla_prep_kernel,
        grid=(m // tm,),
        in_specs=[row(D_MODEL)] + [full(a) for a in consts] + [row(LANES)] * 3,
        out_specs=[row(MLA_HEADS * HEAD_PAD), row(MLA_HEADS * HEAD_PAD), row(MLA_HEADS * V_HEAD)],
        out_shape=[jax.ShapeDtypeStruct((m, MLA_HEADS * HEAD_PAD), BF16),
                   jax.ShapeDtypeStruct((m, MLA_HEADS * HEAD_PAD), BF16),
                   jax.ShapeDtypeStruct((m, MLA_HEADS * V_HEAD), BF16)],
        compiler_params=_params("parallel"),
        name="mla_prep",
    )(x2, *consts, *tabs)


def _attn_kernel(q_ref, k_ref, v_ref, z_ref, o_ref, m_ref, l_ref, acc_ref, *, tq, tk):
    qi = pl.program_id(2)
    q = q_ref[...]
    m_ref[...] = jnp.full(m_ref.shape, -jnp.inf, F32)
    l_ref[...] = jnp.zeros(l_ref.shape, F32)
    acc_ref[...] = jnp.zeros(acc_ref.shape, F32)

    def step(kj, masked):
        start = pl.multiple_of(kj * tk, tk)
        s = _dot_nt(q, k_ref[pl.ds(start, tk), :])
        if masked:
            qpos = qi * tq + lax.broadcasted_iota(jnp.int32, (tq, tk), 0)
            kpos = kj * tk + lax.broadcasted_iota(jnp.int32, (tq, tk), 1)
            s = jnp.where(kpos <= qpos, s, -jnp.inf)
        m_prev = m_ref[...]
        m_new = jnp.maximum(m_prev, jnp.max(s, axis=-1, keepdims=True))
        alpha = jnp.exp(m_prev - m_new)
        p = jnp.exp(s - m_new)
        l_ref[...] = alpha * l_ref[...] + jnp.sum(p, axis=-1, keepdims=True)
        acc_ref[...] = alpha * acc_ref[...] + _dot(p.astype(BF16), v_ref[pl.ds(start, tk), :])
        m_ref[...] = m_new

    n_diag = tq // tk
    lax.fori_loop(0, qi * n_diag, lambda kj, c: (step(kj, False), c)[1], 0)
    for d in range(n_diag):
        step(qi * n_diag + d, True)
    o_ref[...] = (acc_ref[...] / l_ref[...] * z_ref[...].astype(F32)).astype(BF16)


def _attention(q, k, v, proj, batch, seq, tq, tk):
    m = q.shape[0]
    nq = seq // tq
    z_col = COL_MZ * (BRANCH_W // V_HEAD)
    return pl.pallas_call(
        functools.partial(_attn_kernel, tq=tq, tk=tk),
        grid=(batch, MLA_HEADS, nq),
        in_specs=[
            pl.BlockSpec((tq, HEAD_PAD), lambda b, h, i: (b * nq + i, h)),
            pl.BlockSpec((seq, HEAD_PAD), lambda b, h, i: (b, h)),
            pl.BlockSpec((seq, V_HEAD), lambda b, h, i: (b, h)),
            pl.BlockSpec((tq, V_HEAD), lambda b, h, i: (b * nq + i, z_col + h)),
        ],
        out_specs=pl.BlockSpec((tq, V_HEAD), lambda b, h, i: (b * nq + i, h)),
        out_shape=jax.ShapeDtypeStruct((m, MLA_HEADS * V_HEAD), BF16),
        scratch_shapes=[pltpu.VMEM((tq, 1), F32), pltpu.VMEM((tq, 1), F32), pltpu.VMEM((tq, V_HEAD), F32)],
        compiler_params=_params("parallel", "parallel", "arbitrary"),
        name="mla_attention",
    )(q, k, v, proj)


def _mem_kv_kernel(mem_ref, g_ref, w_ref, kg_ref, k_out, v_out):
    mm = mem_ref[...]
    kvm = _dot((mm * _rms(mm, D_MODEL) * g_ref[...]).astype(BF16), w_ref[...])
    for hd in range(XATTN_HEADS):
        kh = kvm[:, hd * XATTN_HEAD_DIM:(hd + 1) * XATTN_HEAD_DIM]
        k_out[:, hd * XATTN_HEAD_DIM:(hd + 1) * XATTN_HEAD_DIM] = (
            kh * _rms(kh, XATTN_HEAD_DIM) * kg_ref[...]).astype(BF16)
    v_out[...] = kvm[:, BRANCH_W:].astype(BF16)


def _mem_kv(mem2, mem_g, w_mem, kg):
    m = mem2.shape[0]
    out = jax.ShapeDtypeStruct((m, BRANCH_W), BF16)
    return pl.pallas_call(
        _mem_kv_kernel,
        grid=(m // MEM_LEN,),
        in_specs=[
            pl.BlockSpec((MEM_LEN, D_MODEL), lambda b: (b, 0)),
            pl.BlockSpec((1, D_MODEL), lambda b: (0, 0)),
            pl.BlockSpec((D_MODEL, 2 * BRANCH_W), lambda b: (0, 0)),
            pl.BlockSpec((1, XATTN_HEAD_DIM), lambda b: (0, 0)),
        ],
        out_specs=[pl.BlockSpec((MEM_LEN, BRANCH_W), lambda b: (b, 0))] * 2,
        out_shape=[out, out],
        compiler_params=_params("parallel"),
        name="mem_kv",
    )(mem2, mem_g, w_mem, kg)


def _mixers_kernel(pv_ref, cb_ref, cc_ref, cx_ref, xq_ref, pz_ref, cz_ref, xz_ref,
                   pvh_ref, cch_ref, cxh_ref, pw_ref, ps_ref, cw_ref, xqg_ref, km_ref, vm_ref,
                   ypool_ref, yconv_ref, ymem_ref, *, tm, tiles_per_seq):
    t0 = (pl.program_id(0) % tiles_per_seq) * tm
    has_prev = t0 > 0
    row = lax.broadcasted_iota(jnp.int32, (tm, 1), 0)

    r_i = lax.broadcasted_iota(jnp.int32, (tm, tm), 0)
    c_i = lax.broadcasted_iota(jnp.int32, (tm, tm), 1)
    rh_i = lax.broadcasted_iota(jnp.int32, (tm, HALO), 0)
    ch_i = lax.broadcasted_iota(jnp.int32, (tm, HALO), 1)
    pos = (t0 + row).astype(F32)
    for g, win in enumerate(POOL_WINDOWS):
        sl = slice(g * POOL_GW, (g + 1) * POOL_GW)
        vg = pv_ref[:, sl]
        band = ((c_i <= r_i) & (r_i - c_i < win)).astype(BF16)
        band_h = ((rh_i + (HALO - ch_i) < win) & has_prev).astype(BF16)
        wsum = _dot(band, vg) + _dot(band_h, pvh_ref[:, sl])
        cnt = jnp.minimum(pos + 1.0, float(win))
        mixed = wsum / cnt - vg.astype(F32)
        out = _dot(mixed.astype(BF16), pw_ref[g])
        ypool_ref[:, sl] = (out * ps_ref[:, sl] * pz_ref[:, sl].astype(F32)).astype(BF16)

    u = cc_ref[...].astype(F32) * cx_ref[...].astype(F32)
    uh = cch_ref[...].astype(F32) * cxh_ref[...].astype(F32)
    uh = jnp.where(has_prev, uh, 0.0)
    u1 = jnp.where(row == 0, uh[HALO - 1:HALO, :], pltpu.roll(u, 1, 0))
    u2 = jnp.where(row == 0, uh[HALO - 2:HALO - 1, :],
                   jnp.where(row == 1, uh[HALO - 1:HALO, :], pltpu.roll(u, 2, 0)))
    y = cw_ref[0:1, :] * u2 + cw_ref[1:2, :] * u1 + cw_ref[2:3, :] * u
    yconv_ref[...] = (cb_ref[...].astype(F32) * y * cz_ref[...].astype(F32)).astype(BF16)

    scale = XATTN_HEAD_DIM ** -0.5
    for hd in range(XATTN_HEADS):
        sl = slice(hd * XATTN_HEAD_DIM, (hd + 1) * XATTN_HEAD_DIM)
        qh = xq_ref[:, sl].astype(F32)
        qn = (qh * (_rms(qh, XATTN_HEAD_DIM) * scale) * xqg_ref[...]).astype(BF16)
        s = _dot_nt(qn, km_ref[:, sl])
        p = jnp.exp(s - jnp.max(s, axis=-1, keepdims=True))
        o = _dot(p.astype(BF16), vm_ref[:, sl]) / jnp.sum(p, axis=-1, keepdims=True)
        ymem_ref[:, sl] = (o * xz_ref[:, sl].astype(F32)).astype(BF16)


def _mixers(proj, pool_w, pool_scale, conv_w, xq_g, k_mem, v_mem, seq, tm):
    m = proj.shape[0]
    tiles_per_seq = seq // tm
    col = lambda c: pl.BlockSpec((tm, BRANCH_W), lambda i: (i, c))
    halo = lambda c: pl.BlockSpec((HALO, BRANCH_W), lambda i: (jnp.maximum(i * (tm // HALO) - 1, 0), c))
    full = lambda a: pl.BlockSpec(a.shape, lambda i: (0,) * a.ndim)
    memb = pl.BlockSpec((MEM_LEN, BRANCH_W), lambda i: (i // tiles_per_seq, 0))
    out = jax.ShapeDtypeStruct((m, BRANCH_W), BF16)
    return pl.pallas_call(
        functools.partial(_mixers_kernel, tm=tm, tiles_per_seq=tiles_per_seq),
        grid=(m // tm,),
        in_specs=[col(COL_PV), col(COL_CB), col(COL_CC), col(COL_CX), col(COL_XQ), col(COL_PZ), col(COL_CZ),
                  col(COL_XZ), halo(COL_PV), halo(COL_CC), halo(COL_CX),
                  full(pool_w), full(pool_scale), full(conv_w), full(xq_g), memb, memb],
        out_specs=[pl.BlockSpec((tm, BRANCH_W), lambda i: (i, 0))] * 3,
        out_shape=[out, out, out],
        compiler_params=_params("parallel"),
        name="mixers",
    )(*([proj] * 11), pool_w, pool_scale, conv_w, xq_g, k_mem, v_mem)


def _merge_kernel(y0_ref, y1_ref, y2_ref, y3_ref, g0_ref, g1_ref, g2_ref, g3_ref, wb_ref, o_ref):
    acc = None
    for b, (y_ref, g_ref) in enumerate(((y0_ref, g0_ref), (y1_ref, g1_ref), (y2_ref, g2_ref), (y3_ref, g3_ref))):
        term = g_ref[...].astype(F32) * _dot(y_ref[...], wb_ref[b])
        acc = term if acc is None else acc + term
    o_ref[...] = acc.astype(BF16)


def _merge(ys, proj, w_branch, tm, tn):
    m = proj.shape[0]
    gate0 = COL_GATE * BRANCH_W // tn
    gate = lambda b: pl.BlockSpec((tm, tn), lambda i, j: (i, gate0 + b * (D_MODEL // tn) + j))
    return pl.pallas_call(
        _merge_kernel,
        grid=(m // tm, D_MODEL // tn),
        in_specs=[pl.BlockSpec((tm, BRANCH_W), lambda i, j: (i, 0))] * N_BRANCH
        + [gate(b) for b in range(N_BRANCH)]
        + [pl.BlockSpec((N_BRANCH, BRANCH_W, tn), lambda i, j: (0, 0, j))],
        out_specs=pl.BlockSpec((tm, tn), lambda i, j: (i, j)),
        out_shape=jax.ShapeDtypeStruct((m, D_MODEL), BF16),
        compiler_params=_params("parallel", "arbitrary"),
        name="gated_merge",
    )(*ys, *([proj] * N_BRANCH), w_branch)


def _out_proj_kernel(x_ref, a_ref, w_ref, o_ref):
    o_ref[...] = x_ref[...] + _dot(a_ref[...], w_ref[...])


def _out_proj(x2, merged, w_out, tm, tn):
    m = x2.shape[0]
    return pl.pallas_call(
        _out_proj_kernel,
        grid=(m // tm, D_MODEL // tn),
        in_specs=[
            pl.BlockSpec((tm, tn), lambda i, j: (i, j)),
            pl.BlockSpec((tm, D_MODEL), lambda i, j: (i, 0)),
            pl.BlockSpec((D_MODEL, tn), lambda i, j: (0, j)),
        ],
        out_specs=pl.BlockSpec((tm, tn), lambda i, j: (i, j)),
        out_shape=jax.ShapeDtypeStruct((m, D_MODEL), F32),
        compiler_params=_params("parallel", "arbitrary"),
        name="out_proj",
    )(x2, merged, w_out)


def _layer_weights(l, norm_g, w_in, gate_b, pool_w, pool_scale, q_a_norm_g, kv_a_norm_g, w_uq, w_ukv,
                   mla_q_norm_g, mla_k_norm_g, conv_w, mem_norm_g, w_mem_kv, xattn_q_norm_g, xattn_k_norm_g,
                   w_branch, w_out):
    w = w_in[l]
    o = 0
    seg = {}
    for name, width in (("pv", BRANCH_W), ("pz", BRANCH_W), ("cq", Q_LORA), ("ckv", KV_LORA), ("kr", QK_ROPE),
                        ("mz", BRANCH_W), ("cb", BRANCH_W), ("cc", BRANCH_W), ("cx", BRANCH_W), ("cz", BRANCH_W),
                        ("xq", BRANCH_W), ("xz", BRANCH_W), ("gate", N_BRANCH * D_MODEL)):
        seg[name] = w[:, o:o + width]
        o += width
    order = ("pv", "cb", "cc", "cx", "xq", "pz", "mz", "cz", "xz", "gate")
    w_main = jnp.concatenate([seg[n] for n in order], axis=1).astype(BF16)
    b_main = jnp.concatenate([jnp.zeros(((N_PLAIN + N_SILU) * BRANCH_W,), F32), gate_b[l]])[None, :]
    w_lat = jnp.concatenate([seg["cq"], seg["ckv"], seg["kr"], jnp.zeros((D_MODEL, LANES - QK_ROPE), F32)],
                            axis=1).astype(BF16)
    w_uq_p = jnp.pad(w_uq[l].reshape(Q_LORA, MLA_HEADS, QK_HEAD), ((0, 0), (0, 0), (0, HEAD_PAD - QK_HEAD)))
    w_uq_p = w_uq_p.reshape(Q_LORA, MLA_HEADS * HEAD_PAD).astype(BF16)
    w_ukv3 = w_ukv[l].reshape(KV_LORA, MLA_HEADS, QK_NOPE + V_HEAD)
    w_ukv_p = jnp.concatenate([w_ukv3[:, :, :QK_NOPE].reshape(KV_LORA, MLA_HEADS * QK_NOPE),
                               w_ukv3[:, :, QK_NOPE:].reshape(KV_LORA, MLA_HEADS * V_HEAD)], axis=1).astype(BF16)
    qg_pad = jnp.pad(mla_q_norm_g[l], (0, HEAD_PAD - QK_HEAD))[None, :]
    kg_nope = mla_k_norm_g[l][None, :QK_NOPE]
    kg_tail = jnp.pad(mla_k_norm_g[l][QK_NOPE:], (0, LANES - QK_ROPE))[None, :]
    return dict(
        norm_g=norm_g[l][None, :], w_main=w_main, b_main=b_main, w_lat=w_lat,
        qag=q_a_norm_g[l][None, :], kvag=kv_a_norm_g[l][None, :], w_uq_p=w_uq_p, w_ukv_p=w_ukv_p,
        qg_pad=qg_pad, kg_nope=kg_nope, kg_tail=kg_tail,
        pool_w=pool_w[l].astype(BF16), pool_scale=pool_scale[l][None, :], conv_w=conv_w[l],
        mem_g=mem_norm_g[l][None, :], w_mem=w_mem_kv[l].astype(BF16),
        xq_g=xattn_q_norm_g[l][None, :], xk_g=xattn_k_norm_g[l][None, :],
        w_branch=w_branch[l].astype(BF16), w_out=w_out[l].astype(BF16))


def _tile(n, want):
    t = min(n, want)
    assert n % t == 0, (n, want)
    return t


def kernel(x, mem, positions, norm_g, w_in, gate_b, pool_w, pool_scale, q_a_norm_g, kv_a_norm_g, w_uq, w_ukv, mla_q_norm_g, mla_k_norm_g, conv_w, mem_norm_g, w_mem_kv, xattn_q_norm_g, xattn_k_norm_g, w_branch, w_out):
    batch, seq, _ = x.shape
    m = batch * seq
    depth = w_in.shape[0]
    assert mem.shape[1] == MEM_LEN and seq % LANES == 0
    x2 = x.reshape(m, D_MODEL)
    mem2 = mem.reshape(batch * MEM_LEN, D_MODEL)
    t_big = _tile(seq, 1024)
    t_mid = _tile(seq, 512)
    tabs = _rope_tables(positions, t_big)
    for l in range(depth):
        p = _layer_weights(l, norm_g, w_in, gate_b, pool_w, pool_scale, q_a_norm_g, kv_a_norm_g, w_uq, w_ukv,
                           mla_q_norm_g, mla_k_norm_g, conv_w, mem_norm_g, w_mem_kv, xattn_q_norm_g,
                           xattn_k_norm_g, w_branch, w_out)
        proj = _in_proj(x2, p["norm_g"], p["w_main"], p["b_main"], t_big, 1024)
        q, k, v = _mla_prep(x2, p["norm_g"], p["w_lat"], p["qag"], p["kvag"], p["w_uq_p"], p["w_ukv_p"],
                            p["qg_pad"], p["kg_nope"], p["kg_tail"], tabs, t_mid)
        y_mla = _attention(q, k, v, proj, batch, seq, t_mid, t_mid)
        k_mem, v_mem = _mem_kv(mem2, p["mem_g"], p["w_mem"], p["xk_g"])
        y_pool, y_conv, y_mem = _mixers(proj, p["pool_w"], p["pool_scale"], p["conv_w"], p["xq_g"], k_mem, v_mem,
                                        seq, t_mid)
        merged = _merge((y_pool, y_mla, y_conv, y_mem), proj, p["w_branch"], t_big, 512)
        x2 = _out_proj(x2, merged, p["w_out"], t_big, 1024)
    return x2.reshape(batch, seq, D_MODEL)
```

```python
import functools

import jax
import jax.numpy as jnp
from jax import lax
from jax.experimental import pallas as pl
from jax.experimental.pallas import tpu as pltpu

D_MODEL = 2048
MEM_LEN = 256
N_BRANCH = 4
BRANCH_W = 1024
POOL_GROUPS = 4
POOL_WINDOWS = (2, 4, 8, 16)
POOL_GW = BRANCH_W // POOL_GROUPS
MLA_HEADS = 8
Q_LORA = 512
KV_LORA = 512
QK_NOPE = 128
QK_ROPE = 64
QK_HEAD = QK_NOPE + QK_ROPE
V_HEAD = 128
ROPE_THETA = 10000.0
CONV_W = 3
XATTN_HEADS = 4
XATTN_HEAD_DIM = BRANCH_W // XATTN_HEADS
EPS = 1e-6

LANES = 128
HEAD_PAD = 2 * LANES
HALO = 16
LAT_W = Q_LORA + KV_LORA + LANES
VMEM_LIMIT = 56 * 1024 * 1024
LOG2_E = 1.4426950408889634
ATTN_HEADS_PER_STEP = 2

(COL_PV, COL_CB, COL_CC, COL_CX, COL_XQ, COL_PZ, COL_MZ, COL_CZ, COL_XZ, COL_GATE) = range(10)
N_PLAIN, N_SILU = 5, 4
N_MAIN = (N_PLAIN + N_SILU) * BRANCH_W + N_BRANCH * D_MODEL

BF16 = jnp.bfloat16
F32 = jnp.float32


def _params(*sem):
    return pltpu.CompilerParams(dimension_semantics=sem, vmem_limit_bytes=VMEM_LIMIT)


def _sigmoid(v):
    return 1.0 / (1.0 + jnp.exp(-v))


def _rms(v, n):
    return lax.rsqrt(jnp.sum(v * v, axis=-1, keepdims=True) * (1.0 / n) + EPS)


def _dot(a, b):
    return jnp.dot(a, b, preferred_element_type=F32)


def _dot_nt(a, b):
    return lax.dot_general(a, b, (((1,), (1,)), ((), ())), preferred_element_type=F32)


def _rope_table_kernel(pos_ref, inv_ref, c_ref, s1_ref, s2_ref):
    ang = pos_ref[...].astype(F32) * inv_ref[...]
    lane = lax.broadcasted_iota(jnp.int32, ang.shape, 1)
    cos, sin = jnp.cos(ang), jnp.sin(ang)
    c_ref[...] = jnp.where(lane < QK_ROPE, cos, 0.0)
    s1_ref[...] = jnp.where((lane >= QK_ROPE // 2) & (lane < QK_ROPE), sin, 0.0)
    s2_ref[...] = jnp.where(lane < QK_ROPE // 2, -sin, 0.0)


def _rope_tables(positions, tm):
    m = positions.size
    inv = ROPE_THETA ** (-jnp.arange(0, QK_ROPE, 2, dtype=F32) / QK_ROPE)
    inv_tile = jnp.concatenate([inv, inv, jnp.zeros((LANES - QK_ROPE,), F32)])[None, :]
    tab = jax.ShapeDtypeStruct((m, LANES), F32)
    return pl.pallas_call(
        _rope_table_kernel,
        grid=(m // tm,),
        in_specs=[pl.BlockSpec((tm, 1), lambda i: (i, 0)), pl.BlockSpec((1, LANES), lambda i: (0, 0))],
        out_specs=[pl.BlockSpec((tm, LANES), lambda i: (i, 0))] * 3,
        out_shape=[tab, tab, tab],
        compiler_params=_params("parallel"),
        name="rope_tables",
    )(positions.reshape(m, 1), inv_tile)


def _in_proj_kernel(x_ref, g_ref, w_ref, b_ref, o_ref, h_ref):
    j = pl.program_id(1)

    @pl.when(j == 0)
    def _():
        x = x_ref[...]
        h_ref[...] = (x * _rms(x, D_MODEL) * g_ref[...]).astype(BF16)

    @pl.when(j < N_PLAIN)
    def _():
        o_ref[...] = _dot(h_ref[...], w_ref[...]).astype(BF16)

    @pl.when((j >= N_PLAIN) & (j < N_PLAIN + N_SILU))
    def _():
        acc = _dot(h_ref[...], w_ref[...])
        o_ref[...] = (acc * _sigmoid(acc)).astype(BF16)

    @pl.when(j >= N_PLAIN + N_SILU)
    def _():
        acc = _dot(h_ref[...], w_ref[...])
        o_ref[...] = _sigmoid(acc + b_ref[...]).astype(BF16)


def _in_proj(x2, norm_g, w_main, b_main, tm, tn):
    m = x2.shape[0]
    return pl.pallas_call(
        _in_proj_kernel,
        grid=(m // tm, N_MAIN // tn),
        in_specs=[
            pl.BlockSpec((tm, D_MODEL), lambda i, j: (i, 0)),
            pl.BlockSpec((1, D_MODEL), lambda i, j: (0, 0)),
            pl.BlockSpec((D_MODEL, tn), lambda i, j: (0, j)),
            pl.BlockSpec((1, tn), lambda i, j: (0, j)),
        ],
        out_specs=pl.BlockSpec((tm, tn), lambda i, j: (i, j)),
        out_shape=jax.ShapeDtypeStruct((m, N_MAIN), BF16),
        scratch_shapes=[pltpu.VMEM((tm, D_MODEL), BF16)],
        compiler_params=_params("parallel", "arbitrary"),
        name="in_proj",
    )(x2, norm_g, w_main, b_main)


def _rope_tile(t, c, s1, s2):
    return t * c + pltpu.roll(t, QK_ROPE // 2, 1) * s1 + pltpu.roll(t, LANES - QK_ROPE // 2, 1) * s2


def _mla_prep_kernel(x_ref, g_ref, wlat_ref, qag_ref, kvag_ref, wuq_ref, wukv_ref, qg_ref, kgn_ref, kgt_ref,
                     c_ref, s1_ref, s2_ref, qt_out, k_out, vt_out):
    x = x_ref[...]
    h = (x * _rms(x, D_MODEL) * g_ref[...]).astype(BF16)
    lat = _dot(h, wlat_ref[...])
    cq = lat[:, :Q_LORA]
    ckv = lat[:, Q_LORA:Q_LORA + KV_LORA]
    kr = lat[:, Q_LORA + KV_LORA:]
    q = _dot((cq * _rms(cq, Q_LORA) * qag_ref[...]).astype(BF16), wuq_ref[...])
    kv = _dot((ckv * _rms(ckv, KV_LORA) * kvag_ref[...]).astype(BF16), wukv_ref[...])
    c, s1, s2 = c_ref[...], s1_ref[...], s2_ref[...]
    kr_ss = jnp.sum(kr * kr, axis=-1, keepdims=True)
    kr_rot = _rope_tile(kr * kgt_ref[...], c, s1, s2)
    qg = qg_ref[...]
    scale = QK_HEAD ** -0.5 * LOG2_E
    for hd in range(MLA_HEADS):
        qh = q[:, hd * HEAD_PAD:(hd + 1) * HEAD_PAD]
        qn = qh * (_rms(qh, QK_HEAD) * scale) * qg
        qt_out[hd * HEAD_PAD:hd * HEAD_PAD + LANES, :] = qn[:, :LANES].T.astype(BF16)
        qt_out[hd * HEAD_PAD + LANES:(hd + 1) * HEAD_PAD, :] = _rope_tile(qn[:, LANES:], c, s1, s2).T.astype(BF16)
        kn = kv[:, hd * QK_NOPE:(hd + 1) * QK_NOPE]
        rk = lax.rsqrt((jnp.sum(kn * kn, axis=-1, keepdims=True) + kr_ss) * (1.0 / QK_HEAD) + EPS)
        k_out[:, hd * HEAD_PAD:hd * HEAD_PAD + LANES] = (kn * rk * kgn_ref[...]).astype(BF16)
        k_out[:, hd * HEAD_PAD + LANES:(hd + 1) * HEAD_PAD] = (kr_rot * rk).astype(BF16)
        vh = kv[:, MLA_HEADS * QK_NOPE + hd * V_HEAD:MLA_HEADS * QK_NOPE + (hd + 1) * V_HEAD]
        vt_out[hd * V_HEAD:(hd + 1) * V_HEAD, :] = vh.T.astype(BF16)


def _mla_prep(x2, norm_g, w_lat, qag, kvag, w_uq_p, w_ukv_p, qg_pad, kg_nope, kg_tail, tabs, tm):
    m = x2.shape[0]
    row = lambda w: pl.BlockSpec((tm, w), lambda i: (i, 0))
    full = lambda a: pl.BlockSpec(a.shape, lambda i: (0,) * a.ndim)
    col = lambda h: pl.BlockSpec((h, tm), lambda i: (0, i))
    consts = (norm_g, w_lat, qag, kvag, w_uq_p, w_ukv_p, qg_pad, kg_nope, kg_tail)
    return pl.pallas_call(
        _mla_prep_kernel,
        grid=(m // tm,),
        in_specs=[row(D_MODEL)] + [full(a) for a in consts] + [row(LANES)] * 3,
        out_specs=[col(MLA_HEADS * HEAD_PAD), row(MLA_HEADS * HEAD_PAD), col(MLA_HEADS * V_HEAD)],
        out_shape=[jax.ShapeDtypeStruct((MLA_HEADS * HEAD_PAD, m), BF16),
                   jax.ShapeDtypeStruct((m, MLA_HEADS * HEAD_PAD), BF16),
                   jax.ShapeDtypeStruct((MLA_HEADS * V_HEAD, m), BF16)],
        compiler_params=_params("parallel"),
        name="mla_prep",
    )(x2, *consts, *tabs)


def _attn_kernel(qt_ref, k_ref, vt_ref, z_ref, o_ref, acc_ref, *, tq, tk, heads):
    qi = pl.program_id(2)
    acc_ref[...] = jnp.zeros(acc_ref.shape, F32)

    def step(kj, stats, masked):
        start = pl.multiple_of(kj * tk, tk)
        out = []
        for g in range(heads):
            m_prev, l_prev = stats[2 * g], stats[2 * g + 1]
            st = _dot(k_ref[pl.ds(start, tk), g * HEAD_PAD:(g + 1) * HEAD_PAD],
                      qt_ref[g * HEAD_PAD:(g + 1) * HEAD_PAD, :])
            if masked:
                kpos = kj * tk + lax.broadcasted_iota(jnp.int32, (tk, tq), 0)
                qpos = qi * tq + lax.broadcasted_iota(jnp.int32, (tk, tq), 1)
                st = jnp.where(kpos <= qpos, st, -jnp.inf)
            m_new = jnp.maximum(m_prev, jnp.max(st, axis=0, keepdims=True))
            alpha = jnp.exp2(m_prev - m_new)
            pt = jnp.exp2(st - m_new)
            l_new = alpha * l_prev + jnp.sum(pt, axis=0, keepdims=True)
            pv = _dot(vt_ref[g * V_HEAD:(g + 1) * V_HEAD, pl.ds(start, tk)], pt.astype(BF16))
            acc_ref[g] = alpha * acc_ref[g] + pv
            out += [m_new, l_new]
        return tuple(out)

    init = (jnp.full((1, tq), -jnp.inf, F32), jnp.zeros((1, tq), F32)) * heads
    n_diag = tq // tk
    stats = lax.fori_loop(0, qi * n_diag, lambda kj, c: step(kj, c, False), init)
    for d in range(n_diag):
        stats = step(qi * n_diag + d, stats, True)
    for g in range(heads):
        o = (acc_ref[g] / stats[2 * g + 1]).T
        o_ref[:, g * V_HEAD:(g + 1) * V_HEAD] = (o * z_ref[:, g * V_HEAD:(g + 1) * V_HEAD].astype(F32)).astype(BF16)


def _attention(qt, k, vt, proj, batch, seq, tq, tk, heads):
    m = k.shape[0]
    nq = seq // tq
    z_col = COL_MZ * BRANCH_W // (heads * V_HEAD)
    return pl.pallas_call(
        functools.partial(_attn_kernel, tq=tq, tk=tk, heads=heads),
        grid=(batch, MLA_HEADS // heads, nq),
        in_specs=[
            pl.BlockSpec((heads * HEAD_PAD, tq), lambda b, h, i: (h, b * nq + i)),
            pl.BlockSpec((seq, heads * HEAD_PAD), lambda b, h, i: (b, h)),
            pl.BlockSpec((heads * V_HEAD, seq), lambda b, h, i: (h, b)),
            pl.BlockSpec((tq, heads * V_HEAD), lambda b, h, i: (b * nq + i, z_col + h)),
        ],
        out_specs=pl.BlockSpec((tq, heads * V_HEAD), lambda b, h, i: (b * nq + i, h)),
        out_shape=jax.ShapeDtypeStruct((m, MLA_HEADS * V_HEAD), BF16),
        scratch_shapes=[pltpu.VMEM((heads, V_HEAD, tq), F32)],
        compiler_params=_params("parallel", "parallel", "arbitrary"),
        name="mla_attention",
    )(qt, k, vt, proj)


def _mem_kv_kernel(mem_ref, g_ref, w_ref, kg_ref, k_out, v_out):
    mm = mem_ref[...]
    kvm = _dot((mm * _rms(mm, D_MODEL) * g_ref[...]).astype(BF16), w_ref[...])
    for hd in range(XATTN_HEADS):
        kh = kvm[:, hd * XATTN_HEAD_DIM:(hd + 1) * XATTN_HEAD_DIM]
        k_out[:, hd * XATTN_HEAD_DIM:(hd + 1) * XATTN_HEAD_DIM] = (
            kh * _rms(kh, XATTN_HEAD_DIM) * kg_ref[...]).astype(BF16)
    v_out[...] = kvm[:, BRANCH_W:].astype(BF16)


def _mem_kv(mem2, mem_g, w_mem, kg):
    m = mem2.shape[0]
    out = jax.ShapeDtypeStruct((m, BRANCH_W), BF16)
    return pl.pallas_call(
        _mem_kv_kernel,
        grid=(m // MEM_LEN,),
        in_specs=[
            pl.BlockSpec((MEM_LEN, D_MODEL), lambda b: (b, 0)),
            pl.BlockSpec((1, D_MODEL), lambda b: (0, 0)),
            pl.BlockSpec((D_MODEL, 2 * BRANCH_W), lambda b: (0, 0)),
            pl.BlockSpec((1, XATTN_HEAD_DIM), lambda b: (0, 0)),
        ],
        out_specs=[pl.BlockSpec((MEM_LEN, BRANCH_W), lambda b: (b, 0))] * 2,
        out_shape=[out, out],
        compiler_params=_params("parallel"),
        name="mem_kv",
    )(mem2, mem_g, w_mem, kg)


def _mixers_kernel(pv_ref, cb_ref, cc_ref, cx_ref, xq_ref, pz_ref, cz_ref, xz_ref,
                   pvh_ref, cch_ref, cxh_ref, pw_ref, ps_ref, cw_ref, xqg_ref, km_ref, vm_ref,
                   ypool_ref, yconv_ref, ymem_ref, *, tm, tiles_per_seq):
    t0 = (pl.program_id(0) % tiles_per_seq) * tm
    has_prev = t0 > 0
    row = lax.broadcasted_iota(jnp.int32, (tm, 1), 0)

    r_i = lax.broadcasted_iota(jnp.int32, (tm, tm), 0)
    c_i = lax.broadcasted_iota(jnp.int32, (tm, tm), 1)
    rh_i = lax.broadcasted_iota(jnp.int32, (tm, HALO), 0)
    ch_i = lax.broadcasted_iota(jnp.int32, (tm, HALO), 1)
    pos = (t0 + row).astype(F32)
    for g, win in enumerate(POOL_WINDOWS):
        sl = slice(g * POOL_GW, (g + 1) * POOL_GW)
        vg = pv_ref[:, sl]
        band = ((c_i <= r_i) & (r_i - c_i < win)).astype(BF16)
        band_h = ((rh_i + (HALO - ch_i) < win) & has_prev).astype(BF16)
        wsum = _dot(band, vg) + _dot(band_h, pvh_ref[:, sl])
        cnt = jnp.minimum(pos + 1.0, float(win))
        mixed = wsum / cnt - vg.astype(F32)
        out = _dot(mixed.astype(BF16), pw_ref[g])
        ypool_ref[:, sl] = (out * ps_ref[:, sl] * pz_ref[:, sl].astype(F32)).astype(BF16)

    u = cc_ref[...].astype(F32) * cx_ref[...].astype(F32)
    uh = cch_ref[...].astype(F32) * cxh_ref[...].astype(F32)
    uh = jnp.where(has_prev, uh, 0.0)
    u1 = jnp.where(row == 0, uh[HALO - 1:HALO, :], pltpu.roll(u, 1, 0))
    u2 = jnp.where(row == 0, uh[HALO - 2:HALO - 1, :],
                   jnp.where(row == 1, uh[HALO - 1:HALO, :], pltpu.roll(u, 2, 0)))
    y = cw_ref[0:1, :] * u2 + cw_ref[1:2, :] * u1 + cw_ref[2:3, :] * u
    yconv_ref[...] = (cb_ref[...].astype(F32) * y * cz_ref[...].astype(F32)).astype(BF16)

    scale = XATTN_HEAD_DIM ** -0.5
    for hd in range(XATTN_HEADS):
        sl = slice(hd * XATTN_HEAD_DIM, (hd + 1) * XATTN_HEAD_DIM)
        qh = xq_ref[:, sl].astype(F32)
        qn = (qh * (_rms(qh, XATTN_HEAD_DIM) * scale) * xqg_ref[...]).astype(BF16)
        s = _dot_nt(qn, km_ref[:, sl])
        p = jnp.exp(s - jnp.max(s, axis=-1, keepdims=True))
        o = _dot(p.astype(BF16), vm_ref[:, sl]) / jnp.sum(p, axis=-1, keepdims=True)
        ymem_ref[:, sl] = (o * xz_ref[:, sl].astype(F32)).astype(BF16)


def _mixers(proj, pool_w, pool_scale, conv_w, xq_g, k_mem, v_mem, seq, tm):
    m = proj.shape[0]
    tiles_per_seq = seq // tm
    col = lambda c: pl.BlockSpec((tm, BRANCH_W), lambda i: (i, c))
    halo = lambda c: pl.BlockSpec((HALO, BRANCH_W), lambda i: (jnp.maximum(i * (tm // HALO) - 1, 0), c))
    full = lambda a: pl.BlockSpec(a.shape, lambda i: (0,) * a.ndim)
    memb = pl.BlockSpec((MEM_LEN, BRANCH_W), lambda i: (i // tiles_per_seq, 0))
    out = jax.ShapeDtypeStruct((m, BRANCH_W), BF16)
    return pl.pallas_call(
        functools.partial(_mixers_kernel, tm=tm, tiles_per_seq=tiles_per_seq),
        grid=(m // tm,),
        in_specs=[col(COL_PV), col(COL_CB), col(COL_CC), col(COL_CX), col(COL_XQ), col(COL_PZ), col(COL_CZ),
                  col(COL_XZ), halo(COL_PV), halo(COL_CC), halo(COL_CX),
                  full(pool_w), full(pool_scale), full(conv_w), full(xq_g), memb, memb],
        out_specs=[pl.BlockSpec((tm, BRANCH_W), lambda i: (i, 0))] * 3,
        out_shape=[out, out, out],
        compiler_params=_params("parallel"),
        name="mixers",
    )(*([proj] * 11), pool_w, pool_scale, conv_w, xq_g, k_mem, v_mem)


def _merge_kernel(y0_ref, y1_ref, y2_ref, y3_ref, g0_ref, g1_ref, g2_ref, g3_ref, wb_ref, o_ref):
    acc = None
    for b, (y_ref, g_ref) in enumerate(((y0_ref, g0_ref), (y1_ref, g1_ref), (y2_ref, g2_ref), (y3_ref, g3_ref))):
        term = g_ref[...].astype(F32) * _dot(y_ref[...], wb_ref[b])
        acc = term if acc is None else acc + term
    o_ref[...] = acc.astype(BF16)


def _merge(ys, proj, w_branch, tm, tn):
    m = proj.shape[0]
    gate0 = COL_GATE * BRANCH_W // tn
    gate = lambda b: pl.BlockSpec((tm, tn), lambda i, j: (i, gate0 + b * (D_MODEL // tn) + j))
    return pl.pallas_call(
        _merge_kernel,
        grid=(m // tm, D_MODEL // tn),
        in_specs=[pl.BlockSpec((tm, BRANCH_W), lambda i, j: (i, 0))] * N_BRANCH
        + [gate(b) for b in range(N_BRANCH)]
        + [pl.BlockSpec((N_BRANCH, BRANCH_W, tn), lambda i, j: (0, 0, j))],
        out_specs=pl.BlockSpec((tm, tn), lambda i, j: (i, j)),
        out_shape=jax.ShapeDtypeStruct((m, D_MODEL), BF16),
        compiler_params=_params("parallel", "arbitrary"),
        name="gated_merge",
    )(*ys, *([proj] * N_BRANCH), w_branch)


def _out_proj_kernel(x_ref, a_ref, w_ref, o_ref):
    o_ref[...] = x_ref[...] + _dot(a_ref[...], w_ref[...])


def _out_proj(x2, merged, w_out, tm, tn):
    m = x2.shape[0]
    return pl.pallas_call(
        _out_proj_kernel,
        grid=(m // tm, D_MODEL // tn),
        in_specs=[
            pl.BlockSpec((tm, tn), lambda i, j: (i, j)),
            pl.BlockSpec((tm, D_MODEL), lambda i, j: (i, 0)),
            pl.BlockSpec((D_MODEL, tn), lambda i, j: (0, j)),
        ],
        out_specs=pl.BlockSpec((tm, tn), lambda i, j: (i, j)),
        out_shape=jax.ShapeDtypeStruct((m, D_MODEL), F32),
        compiler_params=_params("parallel", "arbitrary"),
        name="out_proj",
    )(x2, merged, w_out)


def _layer_weights(l, norm_g, w_in, gate_b, pool_w, pool_scale, q_a_norm_g, kv_a_norm_g, w_uq, w_ukv,
                   mla_q_norm_g, mla_k_norm_g, conv_w, mem_norm_g, w_mem_kv, xattn_q_norm_g, xattn_k_norm_g,
                   w_branch, w_out):
    w = w_in[l]
    o = 0
    seg = {}
    for name, width in (("pv", BRANCH_W), ("pz", BRANCH_W), ("cq", Q_LORA), ("ckv", KV_LORA), ("kr", QK_ROPE),
                        ("mz", BRANCH_W), ("cb", BRANCH_W), ("cc", BRANCH_W), ("cx", BRANCH_W), ("cz", BRANCH_W),
                        ("xq", BRANCH_W), ("xz", BRANCH_W), ("gate", N_BRANCH * D_MODEL)):
        seg[name] = w[:, o:o + width]
        o += width
    order = ("pv", "cb", "cc", "cx", "xq", "pz", "mz", "cz", "xz", "gate")
    w_main = jnp.concatenate([seg[n] for n in order], axis=1).astype(BF16)
    b_main = jnp.concatenate([jnp.zeros(((N_PLAIN + N_SILU) * BRANCH_W,), F32), gate_b[l]])[None, :]
    w_lat = jnp.concatenate([seg["cq"], seg["ckv"], seg["kr"], jnp.zeros((D_MODEL, LANES - QK_ROPE), F32)],
                            axis=1).astype(BF16)
    w_uq_p = jnp.pad(w_uq[l].reshape(Q_LORA, MLA_HEADS, QK_HEAD), ((0, 0), (0, 0), (0, HEAD_PAD - QK_HEAD)))
    w_uq_p = w_uq_p.reshape(Q_LORA, MLA_HEADS * HEAD_PAD).astype(BF16)
    w_ukv3 = w_ukv[l].reshape(KV_LORA, MLA_HEADS, QK_NOPE + V_HEAD)
    w_ukv_p = jnp.concatenate([w_ukv3[:, :, :QK_NOPE].reshape(KV_LORA, MLA_HEADS * QK_NOPE),
                               w_ukv3[:, :, QK_NOPE:].reshape(KV_LORA, MLA_HEADS * V_HEAD)], axis=1).astype(BF16)
    qg_pad = jnp.pad(mla_q_norm_g[l], (0, HEAD_PAD - QK_HEAD))[None, :]
    kg_nope = mla_k_norm_g[l][None, :QK_NOPE]
    kg_tail = jnp.pad(mla_k_norm_g[l][QK_NOPE:], (0, LANES - QK_ROPE))[None, :]
    return dict(
        norm_g=norm_g[l][None, :], w_main=w_main, b_main=b_main, w_lat=w_lat,
        qag=q_a_norm_g[l][None, :], kvag=kv_a_norm_g[l][None, :], w_uq_p=w_uq_p, w_ukv_p=w_ukv_p,
        qg_pad=qg_pad, kg_nope=kg_nope, kg_tail=kg_tail,
        pool_w=pool_w[l].astype(BF16), pool_scale=pool_scale[l][None, :], conv_w=conv_w[l],
        mem_g=mem_norm_g[l][None, :], w_mem=w_mem_kv[l].astype(BF16),
        xq_g=xattn_q_norm_g[l][None, :], xk_g=xattn_k_norm_g[l][None, :],
        w_branch=w_branch[l].astype(BF16), w_out=w_out[l].astype(BF16))


def _tile(n, want):
    t = min(n, want)
    assert n % t == 0, (n, want)
    return t


def kernel(x, mem, positions, norm_g, w_in, gate_b, pool_w, pool_scale, q_a_norm_g, kv_a_norm_g, w_uq, w_ukv, mla_q_norm_g, mla_k_norm_g, conv_w, mem_norm_g, w_mem_kv, xattn_q_norm_g, xattn_k_norm_g, w_branch, w_out):
    batch, seq, _ = x.shape
    m = batch * seq
    depth = w_in.shape[0]
    assert mem.shape[1] == MEM_LEN and seq % LANES == 0
    x2 = x.reshape(m, D_MODEL)
    mem2 = mem.reshape(batch * MEM_LEN, D_MODEL)
    t_big = _tile(seq, 1024)
    t_mid = _tile(seq, 512)
    tabs = _rope_tables(positions, t_big)
    for l in range(depth):
        p = _layer_weights(l, norm_g, w_in, gate_b, pool_w, pool_scale, q_a_norm_g, kv_a_norm_g, w_uq, w_ukv,
                           mla_q_norm_g, mla_k_norm_g, conv_w, mem_norm_g, w_mem_kv, xattn_q_norm_g,
                           xattn_k_norm_g, w_branch, w_out)
        proj = _in_proj(x2, p["norm_g"], p["w_main"], p["b_main"], t_big, 1024)
        qt, k, vt = _mla_prep(x2, p["norm_g"], p["w_lat"], p["qag"], p["kvag"], p["w_uq_p"], p["w_ukv_p"],
                              p["qg_pad"], p["kg_nope"], p["kg_tail"], tabs, t_mid)
        y_mla = _attention(qt, k, vt, proj, batch, seq, t_big, t_big, ATTN_HEADS_PER_STEP)
        k_mem, v_mem = _mem_kv(mem2, p["mem_g"], p["w_mem"], p["xk_g"])
        y_pool, y_conv, y_mem = _mixers(proj, p["pool_w"], p["pool_scale"], p["conv_w"], p["xq_g"], k_mem, v_mem,
                                        seq, t_mid)
        merged = _merge((y_pool, y_mla, y_conv, y_mem), proj, p["w_branch"], t_big, 512)
        x2 = _out_proj(x2, merged, p["w_out"], t_big, 1024)
    return x2.reshape(batch, seq, D_MODEL)
```

```python
import functools

import jax
import jax.numpy as jnp
from jax import lax
from jax.experimental import pallas as pl
from jax.experimental.pallas import tpu as pltpu

D_MODEL = 2048
MEM_LEN = 256
N_BRANCH = 4
BRANCH_W = 1024
POOL_GROUPS = 4
POOL_WINDOWS = (2, 4, 8, 16)
POOL_GW = BRANCH_W // POOL_GROUPS
MLA_HEADS = 8
Q_LORA = 512
KV_LORA = 512
QK_NOPE = 128
QK_ROPE = 64
QK_HEAD = QK_NOPE + QK_ROPE
V_HEAD = 128
ROPE_THETA = 10000.0
CONV_W = 3
XATTN_HEADS = 4
XATTN_HEAD_DIM = BRANCH_W // XATTN_HEADS
EPS = 1e-6

LANES = 128
HEAD_PAD = 2 * LANES
HALO = 16
LAT_W = Q_LORA + KV_LORA + LANES
VMEM_LIMIT = 56 * 1024 * 1024
LOG2_E = 1.4426950408889634
ATTN_HEADS_PER_STEP = 2

(COL_PV, COL_CB, COL_CC, COL_CX, COL_XQ, COL_PZ, COL_MZ, COL_CZ, COL_XZ, COL_GATE) = range(10)
N_PLAIN, N_SILU = 5, 4
N_MAIN = (N_PLAIN + N_SILU) * BRANCH_W + N_BRANCH * D_MODEL

BF16 = jnp.bfloat16
F32 = jnp.float32


def _params(*sem):
    return pltpu.CompilerParams(dimension_semantics=sem, vmem_limit_bytes=VMEM_LIMIT)


def _sigmoid(v):
    return 1.0 / (1.0 + jnp.exp(-v))


def _rms(v, n):
    return lax.rsqrt(jnp.sum(v * v, axis=-1, keepdims=True) * (1.0 / n) + EPS)


def _dot(a, b):
    return jnp.dot(a, b, preferred_element_type=F32)


def _dot_nt(a, b):
    return lax.dot_general(a, b, (((1,), (1,)), ((), ())), preferred_element_type=F32)


def _rope_table_kernel(pos_ref, inv_ref, c_ref, s1_ref, s2_ref):
    ang = pos_ref[...].astype(F32) * inv_ref[...]
    lane = lax.broadcasted_iota(jnp.int32, ang.shape, 1)
    cos, sin = jnp.cos(ang), jnp.sin(ang)
    c_ref[...] = jnp.where(lane < QK_ROPE, cos, 0.0)
    s1_ref[...] = jnp.where((lane >= QK_ROPE // 2) & (lane < QK_ROPE), sin, 0.0)
    s2_ref[...] = jnp.where(lane < QK_ROPE // 2, -sin, 0.0)


def _rope_tables(positions, tm):
    m = positions.size
    inv = ROPE_THETA ** (-jnp.arange(0, QK_ROPE, 2, dtype=F32) / QK_ROPE)
    inv_tile = jnp.concatenate([inv, inv, jnp.zeros((LANES - QK_ROPE,), F32)])[None, :]
    tab = jax.ShapeDtypeStruct((m, LANES), F32)
    return pl.pallas_call(
        _rope_table_kernel,
        grid=(m // tm,),
        in_specs=[pl.BlockSpec((tm, 1), lambda i: (i, 0)), pl.BlockSpec((1, LANES), lambda i: (0, 0))],
        out_specs=[pl.BlockSpec((tm, LANES), lambda i: (i, 0))] * 3,
        out_shape=[tab, tab, tab],
        compiler_params=_params("parallel"),
        name="rope_tables",
    )(positions.reshape(m, 1), inv_tile)


def _in_proj_kernel(x_ref, g_ref, w_ref, b_ref, o_ref, h_ref):
    j = pl.program_id(1)

    @pl.when(j == 0)
    def _():
        x = x_ref[...]
        h_ref[...] = (x * _rms(x, D_MODEL) * g_ref[...]).astype(BF16)

    @pl.when(j < N_PLAIN)
    def _():
        o_ref[...] = _dot(h_ref[...], w_ref[...]).astype(BF16)

    @pl.when((j >= N_PLAIN) & (j < N_PLAIN + N_SILU))
    def _():
        acc = _dot(h_ref[...], w_ref[...])
        o_ref[...] = (acc * _sigmoid(acc)).astype(BF16)

    @pl.when(j >= N_PLAIN + N_SILU)
    def _():
        acc = _dot(h_ref[...], w_ref[...])
        o_ref[...] = _sigmoid(acc + b_ref[...]).astype(BF16)


def _in_proj(x2, norm_g, w_main, b_main, tm, tn):
    m = x2.shape[0]
    return pl.pallas_call(
        _in_proj_kernel,
        grid=(m // tm, N_MAIN // tn),
        in_specs=[
            pl.BlockSpec((tm, D_MODEL), lambda i, j: (i, 0)),
            pl.BlockSpec((1, D_MODEL), lambda i, j: (0, 0)),
            pl.BlockSpec((D_MODEL, tn), lambda i, j: (0, j)),
            pl.BlockSpec((1, tn), lambda i, j: (0, j)),
        ],
        out_specs=pl.BlockSpec((tm, tn), lambda i, j: (i, j)),
        out_shape=jax.ShapeDtypeStruct((m, N_MAIN), BF16),
        scratch_shapes=[pltpu.VMEM((tm, D_MODEL), BF16)],
        compiler_params=_params("parallel", "arbitrary"),
        name="in_proj",
    )(x2, norm_g, w_main, b_main)


def _rope_tile(t, c, s1, s2):
    return t * c + pltpu.roll(t, QK_ROPE // 2, 1) * s1 + pltpu.roll(t, LANES - QK_ROPE // 2, 1) * s2


def _mla_prep_kernel(x_ref, g_ref, wlat_ref, qag_ref, kvag_ref, wuq_ref, wukv_ref, qg_ref, kgn_ref, kgt_ref,
                     c_ref, s1_ref, s2_ref, qt_out, k_out, vt_out, *, n_split):
    rows_per = x_ref.shape[0] // n_split
    for part in range(n_split):
        rows = slice(part * rows_per, (part + 1) * rows_per)
        x = x_ref[rows, :]
        h = (x * _rms(x, D_MODEL) * g_ref[...]).astype(BF16)
        lat = _dot(h, wlat_ref[...])
        cq = lat[:, :Q_LORA]
        ckv = lat[:, Q_LORA:Q_LORA + KV_LORA]
        kr = lat[:, Q_LORA + KV_LORA:]
        q = _dot((cq * _rms(cq, Q_LORA) * qag_ref[...]).astype(BF16), wuq_ref[...])
        kv = _dot((ckv * _rms(ckv, KV_LORA) * kvag_ref[...]).astype(BF16), wukv_ref[...])
        c, s1, s2 = c_ref[rows, :], s1_ref[rows, :], s2_ref[rows, :]
        kr_ss = jnp.sum(kr * kr, axis=-1, keepdims=True)
        kr_rot = _rope_tile(kr * kgt_ref[...], c, s1, s2)
        qg = qg_ref[...]
        scale = QK_HEAD ** -0.5 * LOG2_E
        for hd in range(MLA_HEADS):
            qh = q[:, hd * HEAD_PAD:(hd + 1) * HEAD_PAD]
            qn = qh * (_rms(qh, QK_HEAD) * scale) * qg
            qt_out[hd * HEAD_PAD:hd * HEAD_PAD + LANES, rows] = qn[:, :LANES].T.astype(BF16)
            qt_out[hd * HEAD_PAD + LANES:(hd + 1) * HEAD_PAD, rows] = (
                _rope_tile(qn[:, LANES:], c, s1, s2).T.astype(BF16))
            kn = kv[:, hd * QK_NOPE:(hd + 1) * QK_NOPE]
            rk = lax.rsqrt((jnp.sum(kn * kn, axis=-1, keepdims=True) + kr_ss) * (1.0 / QK_HEAD) + EPS)
            k_out[rows, hd * HEAD_PAD:hd * HEAD_PAD + LANES] = (kn * rk * kgn_ref[...]).astype(BF16)
            k_out[rows, hd * HEAD_PAD + LANES:(hd + 1) * HEAD_PAD] = (kr_rot * rk).astype(BF16)
            vh = kv[:, MLA_HEADS * QK_NOPE + hd * V_HEAD:MLA_HEADS * QK_NOPE + (hd + 1) * V_HEAD]
            vt_out[hd * V_HEAD:(hd + 1) * V_HEAD, rows] = vh.T.astype(BF16)


def _mla_prep(x2, norm_g, w_lat, qag, kvag, w_uq_p, w_ukv_p, qg_pad, kg_nope, kg_tail, tabs, tm):
    m = x2.shape[0]
    row = lambda w: pl.BlockSpec((tm, w), lambda i: (i, 0))
    full = lambda a: pl.BlockSpec(a.shape, lambda i: (0,) * a.ndim)
    col = lambda h: pl.BlockSpec((h, tm), lambda i: (0, i))
    consts = (norm_g, w_lat, qag, kvag, w_uq_p, w_ukv_p, qg_pad, kg_nope, kg_tail)
    return pl.pallas_call(
        functools.partial(_mla_prep_kernel, n_split=2),
        grid=(m // tm,),
        in_specs=[row(D_MODEL)] + [full(a) for a in consts] + [row(LANES)] * 3,
        out_specs=[col(MLA_HEADS * HEAD_PAD), row(MLA_HEADS * HEAD_PAD), col(MLA_HEADS * V_HEAD)],
        out_shape=[jax.ShapeDtypeStruct((MLA_HEADS * HEAD_PAD, m), BF16),
                   jax.ShapeDtypeStruct((m, MLA_HEADS * HEAD_PAD), BF16),
                   jax.ShapeDtypeStruct((MLA_HEADS * V_HEAD, m), BF16)],
        compiler_params=_params("parallel"),
        name="mla_prep",
    )(x2, *consts, *tabs)


def _attn_kernel(qt_ref, k_ref, vt_ref, z_ref, o_ref, acc_ref, sa_ref, sb_ref, *, t, heads):
    qi = pl.program_id(2)
    acc_ref[...] = jnp.zeros(acc_ref.shape, F32)

    def scores(kj, s_ref):
        start = pl.multiple_of(kj * t, t)
        for g in range(heads):
            s_ref[g] = _dot(k_ref[pl.ds(start, t), g * HEAD_PAD:(g + 1) * HEAD_PAD],
                            qt_ref[g * HEAD_PAD:(g + 1) * HEAD_PAD, :])

    def consume(kj, s_ref, stats, masked):
        start = pl.multiple_of(kj * t, t)
        out = []
        for g in range(heads):
            m_prev, l_prev = stats[2 * g], stats[2 * g + 1]
            st = s_ref[g]
            if masked:
                kpos = kj * t + lax.broadcasted_iota(jnp.int32, (t, t), 0)
                qpos = qi * t + lax.broadcasted_iota(jnp.int32, (t, t), 1)
                st = jnp.where(kpos <= qpos, st, -jnp.inf)
            m_new = jnp.maximum(m_prev, jnp.max(st, axis=0, keepdims=True))
            alpha = jnp.exp2(m_prev - m_new)
            pt = jnp.exp2(st - m_new)
            l_new = alpha * l_prev + jnp.sum(pt, axis=0, keepdims=True)
            pv = _dot(vt_ref[g * V_HEAD:(g + 1) * V_HEAD, pl.ds(start, t)], pt.astype(BF16))
            acc_ref[g] = alpha * acc_ref[g] + pv
            out += [m_new, l_new]
        return tuple(out)

    def pair(i, stats):
        scores(2 * i + 1, sb_ref)
        stats = consume(2 * i, sa_ref, stats, False)
        scores(2 * i + 2, sa_ref)
        return consume(2 * i + 1, sb_ref, stats, False)

    init = (jnp.full((1, t), -jnp.inf, F32), jnp.zeros((1, t), F32)) * heads
    scores(0, sa_ref)
    stats = lax.fori_loop(0, qi // 2, pair, init)

    def tail_odd(stats):
        scores(qi, sb_ref)
        return consume(qi, sb_ref, consume(qi - 1, sa_ref, stats, False), True)

    stats = lax.cond(qi % 2 == 1, tail_odd, lambda stats: consume(qi, sa_ref, stats, True), stats)
    for g in range(heads):
        o = (acc_ref[g] / stats[2 * g + 1]).T
        o_ref[:, g * V_HEAD:(g + 1) * V_HEAD] = (o * z_ref[:, g * V_HEAD:(g + 1) * V_HEAD].astype(F32)).astype(BF16)


def _attention(qt, k, vt, proj, batch, seq, t, heads):
    m = k.shape[0]
    tq = t
    nq = seq // tq
    z_col = COL_MZ * BRANCH_W // (heads * V_HEAD)
    return pl.pallas_call(
        functools.partial(_attn_kernel, t=t, heads=heads),
        grid=(batch, MLA_HEADS // heads, nq),
        in_specs=[
            pl.BlockSpec((heads * HEAD_PAD, tq), lambda b, h, i: (h, b * nq + i)),
            pl.BlockSpec((seq, heads * HEAD_PAD), lambda b, h, i: (b, h)),
            pl.BlockSpec((heads * V_HEAD, seq), lambda b, h, i: (h, b)),
            pl.BlockSpec((tq, heads * V_HEAD), lambda b, h, i: (b * nq + i, z_col + h)),
        ],
        out_specs=pl.BlockSpec((tq, heads * V_HEAD), lambda b, h, i: (b * nq + i, h)),
        out_shape=jax.ShapeDtypeStruct((m, MLA_HEADS * V_HEAD), BF16),
        scratch_shapes=[pltpu.VMEM((heads, V_HEAD, t), F32),
                        pltpu.VMEM((heads, t, t), F32), pltpu.VMEM((heads, t, t), F32)],
        compiler_params=_params("parallel", "parallel", "arbitrary"),
        name="mla_attention",
    )(qt, k, vt, proj)


def _mem_kv_kernel(mem_ref, g_ref, w_ref, kg_ref, k_out, v_out):
    mm = mem_ref[...]
    kvm = _dot((mm * _rms(mm, D_MODEL) * g_ref[...]).astype(BF16), w_ref[...])
    for hd in range(XATTN_HEADS):
        kh = kvm[:, hd * XATTN_HEAD_DIM:(hd + 1) * XATTN_HEAD_DIM]
        k_out[:, hd * XATTN_HEAD_DIM:(hd + 1) * XATTN_HEAD_DIM] = (
            kh * _rms(kh, XATTN_HEAD_DIM) * kg_ref[...]).astype(BF16)
    v_out[...] = kvm[:, BRANCH_W:].astype(BF16)


def _mem_kv(mem2, mem_g, w_mem, kg):
    m = mem2.shape[0]
    out = jax.ShapeDtypeStruct((m, BRANCH_W), BF16)
    return pl.pallas_call(
        _mem_kv_kernel,
        grid=(m // MEM_LEN,),
        in_specs=[
            pl.BlockSpec((MEM_LEN, D_MODEL), lambda b: (b, 0)),
            pl.BlockSpec((1, D_MODEL), lambda b: (0, 0)),
            pl.BlockSpec((D_MODEL, 2 * BRANCH_W), lambda b: (0, 0)),
            pl.BlockSpec((1, XATTN_HEAD_DIM), lambda b: (0, 0)),
        ],
        out_specs=[pl.BlockSpec((MEM_LEN, BRANCH_W), lambda b: (b, 0))] * 2,
        out_shape=[out, out],
        compiler_params=_params("parallel"),
        name="mem_kv",
    )(mem2, mem_g, w_mem, kg)


def _mixers_kernel(pv_ref, cb_ref, cc_ref, cx_ref, xq_ref, pz_ref, cz_ref, xz_ref,
                   pvh_ref, cch_ref, cxh_ref, pw_ref, ps_ref, cw_ref, xqg_ref, km_ref, vm_ref,
                   ypool_ref, yconv_ref, ymem_ref, *, tm, tiles_per_seq):
    t0 = (pl.program_id(0) % tiles_per_seq) * tm
    has_prev = t0 > 0
    row = lax.broadcasted_iota(jnp.int32, (tm, 1), 0)

    r_i = lax.broadcasted_iota(jnp.int32, (tm, tm), 0)
    c_i = lax.broadcasted_iota(jnp.int32, (tm, tm), 1)
    rh_i = lax.broadcasted_iota(jnp.int32, (tm, HALO), 0)
    ch_i = lax.broadcasted_iota(jnp.int32, (tm, HALO), 1)
    pos = (t0 + row).astype(F32)
    for g, win in enumerate(POOL_WINDOWS):
        sl = slice(g * POOL_GW, (g + 1) * POOL_GW)
        vg = pv_ref[:, sl]
        band = ((c_i <= r_i) & (r_i - c_i < win)).astype(BF16)
        band_h = ((rh_i + (HALO - ch_i) < win) & has_prev).astype(BF16)
        wsum = _dot(band, vg) + _dot(band_h, pvh_ref[:, sl])
        cnt = jnp.minimum(pos + 1.0, float(win))
        mixed = wsum / cnt - vg.astype(F32)
        out = _dot(mixed.astype(BF16), pw_ref[g])
        ypool_ref[:, sl] = (out * ps_ref[:, sl] * pz_ref[:, sl].astype(F32)).astype(BF16)

    u = cc_ref[...].astype(F32) * cx_ref[...].astype(F32)
    uh = cch_ref[...].astype(F32) * cxh_ref[...].astype(F32)
    uh = jnp.where(has_prev, uh, 0.0)
    u1 = jnp.where(row == 0, uh[HALO - 1:HALO, :], pltpu.roll(u, 1, 0))
    u2 = jnp.where(row == 0, uh[HALO - 2:HALO - 1, :],
                   jnp.where(row == 1, uh[HALO - 1:HALO, :], pltpu.roll(u, 2, 0)))
    y = cw_ref[0:1, :] * u2 + cw_ref[1:2, :] * u1 + cw_ref[2:3, :] * u
    yconv_ref[...] = (cb_ref[...].astype(F32) * y * cz_ref[...].astype(F32)).astype(BF16)

    scale = XATTN_HEAD_DIM ** -0.5
    for hd in range(XATTN_HEADS):
        sl = slice(hd * XATTN_HEAD_DIM, (hd + 1) * XATTN_HEAD_DIM)
        qh = xq_ref[:, sl].astype(F32)
        qn = (qh * (_rms(qh, XATTN_HEAD_DIM) * scale) * xqg_ref[...]).astype(BF16)
        s = _dot_nt(qn, km_ref[:, sl])
        p = jnp.exp(s - jnp.max(s, axis=-1, keepdims=True))
        o = _dot(p.astype(BF16), vm_ref[:, sl]) / jnp.sum(p, axis=-1, keepdims=True)
        ymem_ref[:, sl] = (o * xz_ref[:, sl].astype(F32)).astype(BF16)


def _mixers(proj, pool_w, pool_scale, conv_w, xq_g, k_mem, v_mem, seq, tm):
    m = proj.shape[0]
    tiles_per_seq = seq // tm
    col = lambda c: pl.BlockSpec((tm, BRANCH_W), lambda i: (i, c))
    halo = lambda c: pl.BlockSpec((HALO, BRANCH_W), lambda i: (jnp.maximum(i * (tm // HALO) - 1, 0), c))
    full = lambda a: pl.BlockSpec(a.shape, lambda i: (0,) * a.ndim)
    memb = pl.BlockSpec((MEM_LEN, BRANCH_W), lambda i: (i // tiles_per_seq, 0))
    out = jax.ShapeDtypeStruct((m, BRANCH_W), BF16)
    return pl.pallas_call(
        functools.partial(_mixers_kernel, tm=tm, tiles_per_seq=tiles_per_seq),
        grid=(m // tm,),
        in_specs=[col(COL_PV), col(COL_CB), col(COL_CC), col(COL_CX), col(COL_XQ), col(COL_PZ), col(COL_CZ),
                  col(COL_XZ), halo(COL_PV), halo(COL_CC), halo(COL_CX),
                  full(pool_w), full(pool_scale), full(conv_w), full(xq_g), memb, memb],
        out_specs=[pl.BlockSpec((tm, BRANCH_W), lambda i: (i, 0))] * 3,
        out_shape=[out, out, out],
        compiler_params=_params("parallel"),
        name="mixers",
    )(*([proj] * 11), pool_w, pool_scale, conv_w, xq_g, k_mem, v_mem)


def _merge_kernel(y0_ref, y1_ref, y2_ref, y3_ref, g0_ref, g1_ref, g2_ref, g3_ref, wb_ref, o_ref):
    acc = None
    for b, (y_ref, g_ref) in enumerate(((y0_ref, g0_ref), (y1_ref, g1_ref), (y2_ref, g2_ref), (y3_ref, g3_ref))):
        term = g_ref[...].astype(F32) * _dot(y_ref[...], wb_ref[b])
        acc = term if acc is None else acc + term
    o_ref[...] = acc.astype(BF16)


def _merge(ys, proj, w_branch, tm, tn):
    m = proj.shape[0]
    gate0 = COL_GATE * BRANCH_W // tn
    gate = lambda b: pl.BlockSpec((tm, tn), lambda i, j: (i, gate0 + b * (D_MODEL // tn) + j))
    return pl.pallas_call(
        _merge_kernel,
        grid=(m // tm, D_MODEL // tn),
        in_specs=[pl.BlockSpec((tm, BRANCH_W), lambda i, j: (i, 0))] * N_BRANCH
        + [gate(b) for b in range(N_BRANCH)]
        + [pl.BlockSpec((N_BRANCH, BRANCH_W, tn), lambda i, j: (0, 0, j))],
        out_specs=pl.BlockSpec((tm, tn), lambda i, j: (i, j)),
        out_shape=jax.ShapeDtypeStruct((m, D_MODEL), BF16),
        compiler_params=_params("parallel", "arbitrary"),
        name="gated_merge",
    )(*ys, *([proj] * N_BRANCH), w_branch)


def _out_proj_kernel(x_ref, a_ref, w_ref, o_ref):
    o_ref[...] = x_ref[...] + _dot(a_ref[...], w_ref[...])


def _out_proj(x2, merged, w_out, tm, tn):
    m = x2.shape[0]
    return pl.pallas_call(
        _out_proj_kernel,
        grid=(m // tm, D_MODEL // tn),
        in_specs=[
            pl.BlockSpec((tm, tn), lambda i, j: (i, j)),
            pl.BlockSpec((tm, D_MODEL), lambda i, j: (i, 0)),
            pl.BlockSpec((D_MODEL, tn), lambda i, j: (0, j)),
        ],
        out_specs=pl.BlockSpec((tm, tn), lambda i, j: (i, j)),
        out_shape=jax.ShapeDtypeStruct((m, D_MODEL), F32),
        compiler_params=_params("parallel", "arbitrary"),
        name="out_proj",
    )(x2, merged, w_out)


def _layer_weights(l, norm_g, w_in, gate_b, pool_w, pool_scale, q_a_norm_g, kv_a_norm_g, w_uq, w_ukv,
                   mla_q_norm_g, mla_k_norm_g, conv_w, mem_norm_g, w_mem_kv, xattn_q_norm_g, xattn_k_norm_g,
                   w_branch, w_out):
    w = w_in[l]
    o = 0
    seg = {}
    for name, width in (("pv", BRANCH_W), ("pz", BRANCH_W), ("cq", Q_LORA), ("ckv", KV_LORA), ("kr", QK_ROPE),
                        ("mz", BRANCH_W), ("cb", BRANCH_W), ("cc", BRANCH_W), ("cx", BRANCH_W), ("cz", BRANCH_W),
                        ("xq", BRANCH_W), ("xz", BRANCH_W), ("gate", N_BRANCH * D_MODEL)):
        seg[name] = w[:, o:o + width]
        o += width
    order = ("pv", "cb", "cc", "cx", "xq", "pz", "mz", "cz", "xz", "gate")
    w_main = jnp.concatenate([seg[n] for n in order], axis=1).astype(BF16)
    b_main = jnp.concatenate([jnp.zeros(((N_PLAIN + N_SILU) * BRANCH_W,), F32), gate_b[l]])[None, :]
    w_lat = jnp.concatenate([seg["cq"], seg["ckv"], seg["kr"], jnp.zeros((D_MODEL, LANES - QK_ROPE), F32)],
                            axis=1).astype(BF16)
    w_uq_p = jnp.pad(w_uq[l].reshape(Q_LORA, MLA_HEADS, QK_HEAD), ((0, 0), (0, 0), (0, HEAD_PAD - QK_HEAD)))
    w_uq_p = w_uq_p.reshape(Q_LORA, MLA_HEADS * HEAD_PAD).astype(BF16)
    w_ukv3 = w_ukv[l].reshape(KV_LORA, MLA_HEADS, QK_NOPE + V_HEAD)
    w_ukv_p = jnp.concatenate([w_ukv3[:, :, :QK_NOPE].reshape(KV_LORA, MLA_HEADS * QK_NOPE),
                               w_ukv3[:, :, QK_NOPE:].reshape(KV_LORA, MLA_HEADS * V_HEAD)], axis=1).astype(BF16)
    qg_pad = jnp.pad(mla_q_norm_g[l], (0, HEAD_PAD - QK_HEAD))[None, :]
    kg_nope = mla_k_norm_g[l][None, :QK_NOPE]
    kg_tail = jnp.pad(mla_k_norm_g[l][QK_NOPE:], (0, LANES - QK_ROPE))[None, :]
    return dict(
        norm_g=norm_g[l][None, :], w_main=w_main, b_main=b_main, w_lat=w_lat,
        qag=q_a_norm_g[l][None, :], kvag=kv_a_norm_g[l][None, :], w_uq_p=w_uq_p, w_ukv_p=w_ukv_p,
        qg_pad=qg_pad, kg_nope=kg_nope, kg_tail=kg_tail,
        pool_w=pool_w[l].astype(BF16), pool_scale=pool_scale[l][None, :], conv_w=conv_w[l],
        mem_g=mem_norm_g[l][None, :], w_mem=w_mem_kv[l].astype(BF16),
        xq_g=xattn_q_norm_g[l][None, :], xk_g=xattn_k_norm_g[l][None, :],
        w_branch=w_branch[l].astype(BF16), w_out=w_out[l].astype(BF16))


def _tile(n, want):
    t = min(n, want)
    assert n % t == 0, (n, want)
    return t


def kernel(x, mem, positions, norm_g, w_in, gate_b, pool_w, pool_scale, q_a_norm_g, kv_a_norm_g, w_uq, w_ukv, mla_q_norm_g, mla_k_norm_g, conv_w, mem_norm_g, w_mem_kv, xattn_q_norm_g, xattn_k_norm_g, w_branch, w_out):
    batch, seq, _ = x.shape
    m = batch * seq
    depth = w_in.shape[0]
    assert mem.shape[1] == MEM_LEN and seq % LANES == 0
    x2 = x.reshape(m, D_MODEL)
    mem2 = mem.reshape(batch * MEM_LEN, D_MODEL)
    t_big = _tile(seq, 1024)
    t_mid = _tile(seq, 512)
    tabs = _rope_tables(positions, t_big)
    for l in range(depth):
        p = _layer_weights(l, norm_g, w_in, gate_b, pool_w, pool_scale, q_a_norm_g, kv_a_norm_g, w_uq, w_ukv,
                           mla_q_norm_g, mla_k_norm_g, conv_w, mem_norm_g, w_mem_kv, xattn_q_norm_g,
                           xattn_k_norm_g, w_branch, w_out)
        proj = _in_proj(x2, p["norm_g"], p["w_main"], p["b_main"], t_big, 1024)
        qt, k, vt = _mla_prep(x2, p["norm_g"], p["w_lat"], p["qag"], p["kvag"], p["w_uq_p"], p["w_ukv_p"],
                              p["qg_pad"], p["kg_nope"], p["kg_tail"], tabs, t_mid)
        y_mla = _attention(qt, k, vt, proj, batch, seq, t_mid, ATTN_HEADS_PER_STEP)
        k_mem, v_mem = _mem_kv(mem2, p["mem_g"], p["w_mem"], p["xk_g"])
        y_pool, y_conv, y_mem = _mixers(proj, p["pool_w"], p["pool_scale"], p["conv_w"], p["xq_g"], k_mem, v_mem,
                                        seq, t_mid)
        merged = _merge((y_pool, y_mla, y_conv, y_mem), proj, p["w_branch"], t_big, 512)
        x2 = _out_proj(x2, merged, p["w_out"], t_big, 1024)
    return x2.reshape(batch, seq, D_MODEL)
```

```python
import functools

import jax
import jax.numpy as jnp
from jax import lax
from jax.experimental import pallas as pl
from jax.experimental.pallas import tpu as pltpu

D_MODEL = 2048
MEM_LEN = 256
N_BRANCH = 4
BRANCH_W = 1024
POOL_GROUPS = 4
POOL_WINDOWS = (2, 4, 8, 16)
POOL_GW = BRANCH_W // POOL_GROUPS
MLA_HEADS = 8
Q_LORA = 512
KV_LORA = 512
QK_NOPE = 128
QK_ROPE = 64
QK_HEAD = QK_NOPE + QK_ROPE
V_HEAD = 128
ROPE_THETA = 10000.0
CONV_W = 3
XATTN_HEADS = 4
XATTN_HEAD_DIM = BRANCH_W // XATTN_HEADS
EPS = 1e-6

LANES = 128
HEAD_PAD = 2 * LANES
HALO = 16
LAT_W = Q_LORA + KV_LORA + LANES
VMEM_LIMIT = 56 * 1024 * 1024
LOG2_E = 1.4426950408889634
ATTN_HEADS_PER_STEP = 2

(COL_PV, COL_CB, COL_CC, COL_CX, COL_XQ, COL_PZ, COL_MZ, COL_CZ, COL_XZ, COL_GATE) = range(10)
N_PLAIN, N_SILU = 5, 4
N_MAIN = (N_PLAIN + N_SILU) * BRANCH_W + N_BRANCH * D_MODEL

BF16 = jnp.bfloat16
F32 = jnp.float32


def _params(*sem):
    return pltpu.CompilerParams(dimension_semantics=sem, vmem_limit_bytes=VMEM_LIMIT)


def _sigmoid(v):
    return 0.5 * jnp.tanh(0.5 * v) + 0.5


def _rms(v, n):
    return lax.rsqrt(jnp.sum(v * v, axis=-1, keepdims=True) * (1.0 / n) + EPS)


def _dot(a, b):
    return jnp.dot(a, b, preferred_element_type=F32)


def _dot_nt(a, b):
    return lax.dot_general(a, b, (((1,), (1,)), ((), ())), preferred_element_type=F32)


def _rope_table_kernel(pos_ref, inv_ref, c_ref, s1_ref, s2_ref):
    ang = pos_ref[...].astype(F32) * inv_ref[...]
    lane = lax.broadcasted_iota(jnp.int32, ang.shape, 1)
    cos, sin = jnp.cos(ang), jnp.sin(ang)
    c_ref[...] = jnp.where(lane < QK_ROPE, cos, 0.0)
    s1_ref[...] = jnp.where((lane >= QK_ROPE // 2) & (lane < QK_ROPE), sin, 0.0)
    s2_ref[...] = jnp.where(lane < QK_ROPE // 2, -sin, 0.0)


def _rope_tables(positions, tm):
    m = positions.size
    inv = ROPE_THETA ** (-jnp.arange(0, QK_ROPE, 2, dtype=F32) / QK_ROPE)
    inv_tile = jnp.concatenate([inv, inv, jnp.zeros((LANES - QK_ROPE,), F32)])[None, :]
    tab = jax.ShapeDtypeStruct((m, LANES), F32)
    return pl.pallas_call(
        _rope_table_kernel,
        grid=(m // tm,),
        in_specs=[pl.BlockSpec((tm, 1), lambda i: (i, 0)), pl.BlockSpec((1, LANES), lambda i: (0, 0))],
        out_specs=[pl.BlockSpec((tm, LANES), lambda i: (i, 0))] * 3,
        out_shape=[tab, tab, tab],
        compiler_params=_params("parallel"),
        name="rope_tables",
    )(positions.reshape(m, 1), inv_tile)


def _in_proj_kernel(x_ref, g_ref, w_ref, b_ref, o_ref, h_ref):
    j = pl.program_id(1)

    @pl.when(j == 0)
    def _():
        x = x_ref[...]
        h_ref[...] = (x * _rms(x, D_MODEL) * g_ref[...]).astype(BF16)

    @pl.when(j < N_PLAIN)
    def _():
        o_ref[...] = _dot(h_ref[...], w_ref[...]).astype(BF16)

    @pl.when((j >= N_PLAIN) & (j < N_PLAIN + N_SILU))
    def _():
        acc = _dot(h_ref[...], w_ref[...])
        o_ref[...] = (acc * _sigmoid(acc)).astype(BF16)

    @pl.when(j >= N_PLAIN + N_SILU)
    def _():
        acc = _dot(h_ref[...], w_ref[...])
        o_ref[...] = _sigmoid(acc + b_ref[...]).astype(BF16)


def _in_proj(x2, norm_g, w_main, b_main, tm, tn):
    m = x2.shape[0]
    return pl.pallas_call(
        _in_proj_kernel,
        grid=(m // tm, N_MAIN // tn),
        in_specs=[
            pl.BlockSpec((tm, D_MODEL), lambda i, j: (i, 0)),
            pl.BlockSpec((1, D_MODEL), lambda i, j: (0, 0)),
            pl.BlockSpec((D_MODEL, tn), lambda i, j: (0, j)),
            pl.BlockSpec((1, tn), lambda i, j: (0, j)),
        ],
        out_specs=pl.BlockSpec((tm, tn), lambda i, j: (i, j)),
        out_shape=jax.ShapeDtypeStruct((m, N_MAIN), BF16),
        scratch_shapes=[pltpu.VMEM((tm, D_MODEL), BF16)],
        compiler_params=_params("parallel", "arbitrary"),
        name="in_proj",
    )(x2, norm_g, w_main, b_main)


def _rope_tile(t, c, s1, s2):
    return t * c + pltpu.roll(t, QK_ROPE // 2, 1) * s1 + pltpu.roll(t, LANES - QK_ROPE // 2, 1) * s2


def _mla_prep_kernel(x_ref, g_ref, wlat_ref, qag_ref, kvag_ref, wuq_ref, wukv_ref, qg_ref, kgn_ref, kgt_ref,
                     c_ref, s1_ref, s2_ref, qt_out, k_out, vt_out, *, n_split):
    rows_per = x_ref.shape[0] // n_split
    for part in range(n_split):
        rows = slice(part * rows_per, (part + 1) * rows_per)
        x = x_ref[rows, :]
        h = (x * _rms(x, D_MODEL) * g_ref[...]).astype(BF16)
        lat = _dot(h, wlat_ref[...])
        cq = lat[:, :Q_LORA]
        ckv = lat[:, Q_LORA:Q_LORA + KV_LORA]
        kr = lat[:, Q_LORA + KV_LORA:]
        q = _dot((cq * _rms(cq, Q_LORA) * qag_ref[...]).astype(BF16), wuq_ref[...])
        kv = _dot((ckv * _rms(ckv, KV_LORA) * kvag_ref[...]).astype(BF16), wukv_ref[...])
        c, s1, s2 = c_ref[rows, :], s1_ref[rows, :], s2_ref[rows, :]
        kr_ss = jnp.sum(kr * kr, axis=-1, keepdims=True)
        kr_rot = _rope_tile(kr * kgt_ref[...], c, s1, s2)
        qg = qg_ref[...]
        scale = QK_HEAD ** -0.5 * LOG2_E
        for hd in range(MLA_HEADS):
            qh = q[:, hd * HEAD_PAD:(hd + 1) * HEAD_PAD]
            qn = qh * (_rms(qh, QK_HEAD) * scale) * qg
            qt_out[hd * HEAD_PAD:hd * HEAD_PAD + LANES, rows] = qn[:, :LANES].T.astype(BF16)
            qt_out[hd * HEAD_PAD + LANES:(hd + 1) * HEAD_PAD, rows] = (
                _rope_tile(qn[:, LANES:], c, s1, s2).T.astype(BF16))
            kn = kv[:, hd * QK_NOPE:(hd + 1) * QK_NOPE]
            rk = lax.rsqrt((jnp.sum(kn * kn, axis=-1, keepdims=True) + kr_ss) * (1.0 / QK_HEAD) + EPS)
            k_out[rows, hd * HEAD_PAD:hd * HEAD_PAD + LANES] = (kn * rk * kgn_ref[...]).astype(BF16)
            k_out[rows, hd * HEAD_PAD + LANES:(hd + 1) * HEAD_PAD] = (kr_rot * rk).astype(BF16)
            vh = kv[:, MLA_HEADS * QK_NOPE + hd * V_HEAD:MLA_HEADS * QK_NOPE + (hd + 1) * V_HEAD]
            vt_out[hd * V_HEAD:(hd + 1) * V_HEAD, rows] = vh.T.astype(BF16)


def _mla_prep(x2, norm_g, w_lat, qag, kvag, w_uq_p, w_ukv_p, qg_pad, kg_nope, kg_tail, tabs, tm):
    m = x2.shape[0]
    row = lambda w: pl.BlockSpec((tm, w), lambda i: (i, 0))
    full = lambda a: pl.BlockSpec(a.shape, lambda i: (0,) * a.ndim)
    col = lambda h: pl.BlockSpec((h, tm), lambda i: (0, i))
    consts = (norm_g, w_lat, qag, kvag, w_uq_p, w_ukv_p, qg_pad, kg_nope, kg_tail)
    return pl.pallas_call(
        functools.partial(_mla_prep_kernel, n_split=2),
        grid=(m // tm,),
        in_specs=[row(D_MODEL)] + [full(a) for a in consts] + [row(LANES)] * 3,
        out_specs=[col(MLA_HEADS * HEAD_PAD), row(MLA_HEADS * HEAD_PAD), col(MLA_HEADS * V_HEAD)],
        out_shape=[jax.ShapeDtypeStruct((MLA_HEADS * HEAD_PAD, m), BF16),
                   jax.ShapeDtypeStruct((m, MLA_HEADS * HEAD_PAD), BF16),
                   jax.ShapeDtypeStruct((MLA_HEADS * V_HEAD, m), BF16)],
        compiler_params=_params("parallel"),
        name="mla_prep",
    )(x2, *consts, *tabs)


def _attn_kernel(qt_ref, k_ref, vt_ref, z_ref, o_ref, acc_ref, sa_ref, sb_ref, *, t, heads):
    qi = pl.program_id(2)
    acc_ref[...] = jnp.zeros(acc_ref.shape, F32)

    def scores(kj, s_ref):
        start = pl.multiple_of(kj * t, t)
        for g in range(heads):
            s_ref[g] = _dot(k_ref[pl.ds(start, t), g * HEAD_PAD:(g + 1) * HEAD_PAD],
                            qt_ref[g * HEAD_PAD:(g + 1) * HEAD_PAD, :])

    def consume(kj, s_ref, stats, masked):
        start = pl.multiple_of(kj * t, t)
        out = []
        for g in range(heads):
            m_prev, l_prev = stats[2 * g], stats[2 * g + 1]
            st = s_ref[g]
            if masked:
                kpos = kj * t + lax.broadcasted_iota(jnp.int32, (t, t), 0)
                qpos = qi * t + lax.broadcasted_iota(jnp.int32, (t, t), 1)
                st = jnp.where(kpos <= qpos, st, -jnp.inf)
            m_new = jnp.maximum(m_prev, jnp.max(st, axis=0, keepdims=True))
            alpha = jnp.exp2(m_prev - m_new)
            pt = jnp.exp2(st - m_new)
            l_new = alpha * l_prev + jnp.sum(pt, axis=0, keepdims=True)
            pv = _dot(vt_ref[g * V_HEAD:(g + 1) * V_HEAD, pl.ds(start, t)], pt.astype(BF16))
            acc_ref[g] = alpha * acc_ref[g] + pv
            out += [m_new, l_new]
        return tuple(out)

    def pair(i, stats):
        scores(2 * i + 1, sb_ref)
        stats = consume(2 * i, sa_ref, stats, False)
        scores(2 * i + 2, sa_ref)
        return consume(2 * i + 1, sb_ref, stats, False)

    init = (jnp.full((1, t), -jnp.inf, F32), jnp.zeros((1, t), F32)) * heads
    scores(0, sa_ref)
    stats = lax.fori_loop(0, qi // 2, pair, init)

    def tail_odd(stats):
        scores(qi, sb_ref)
        return consume(qi, sb_ref, consume(qi - 1, sa_ref, stats, False), True)

    stats = lax.cond(qi % 2 == 1, tail_odd, lambda stats: consume(qi, sa_ref, stats, True), stats)
    for g in range(heads):
        o = (acc_ref[g] / stats[2 * g + 1]).T
        o_ref[:, g * V_HEAD:(g + 1) * V_HEAD] = (o * z_ref[:, g * V_HEAD:(g + 1) * V_HEAD].astype(F32)).astype(BF16)


def _attention(qt, k, vt, proj, batch, seq, t, heads):
    m = k.shape[0]
    tq = t
    nq = seq // tq
    z_col = COL_MZ * BRANCH_W // (heads * V_HEAD)
    return pl.pallas_call(
        functools.partial(_attn_kernel, t=t, heads=heads),
        grid=(batch, MLA_HEADS // heads, nq),
        in_specs=[
            pl.BlockSpec((heads * HEAD_PAD, tq), lambda b, h, i: (h, b * nq + i)),
            pl.BlockSpec((seq, heads * HEAD_PAD), lambda b, h, i: (b, h)),
            pl.BlockSpec((heads * V_HEAD, seq), lambda b, h, i: (h, b)),
            pl.BlockSpec((tq, heads * V_HEAD), lambda b, h, i: (b * nq + i, z_col + h)),
        ],
        out_specs=pl.BlockSpec((tq, heads * V_HEAD), lambda b, h, i: (b * nq + i, h)),
        out_shape=jax.ShapeDtypeStruct((m, MLA_HEADS * V_HEAD), BF16),
        scratch_shapes=[pltpu.VMEM((heads, V_HEAD, t), F32),
                        pltpu.VMEM((heads, t, t), F32), pltpu.VMEM((heads, t, t), F32)],
        compiler_params=_params("parallel", "parallel", "arbitrary"),
        name="mla_attention",
    )(qt, k, vt, proj)


def _mem_kv_kernel(mem_ref, g_ref, w_ref, kg_ref, k_out, v_out):
    mm = mem_ref[...]
    kvm = _dot((mm * _rms(mm, D_MODEL) * g_ref[...]).astype(BF16), w_ref[...])
    for hd in range(XATTN_HEADS):
        kh = kvm[:, hd * XATTN_HEAD_DIM:(hd + 1) * XATTN_HEAD_DIM]
        k_out[:, hd * XATTN_HEAD_DIM:(hd + 1) * XATTN_HEAD_DIM] = (
            kh * _rms(kh, XATTN_HEAD_DIM) * kg_ref[...]).astype(BF16)
    v_out[...] = kvm[:, BRANCH_W:].astype(BF16)


def _mem_kv(mem2, mem_g, w_mem, kg):
    m = mem2.shape[0]
    out = jax.ShapeDtypeStruct((m, BRANCH_W), BF16)
    return pl.pallas_call(
        _mem_kv_kernel,
        grid=(m // MEM_LEN,),
        in_specs=[
            pl.BlockSpec((MEM_LEN, D_MODEL), lambda b: (b, 0)),
            pl.BlockSpec((1, D_MODEL), lambda b: (0, 0)),
            pl.BlockSpec((D_MODEL, 2 * BRANCH_W), lambda b: (0, 0)),
            pl.BlockSpec((1, XATTN_HEAD_DIM), lambda b: (0, 0)),
        ],
        out_specs=[pl.BlockSpec((MEM_LEN, BRANCH_W), lambda b: (b, 0))] * 2,
        out_shape=[out, out],
        compiler_params=_params("parallel"),
        name="mem_kv",
    )(mem2, mem_g, w_mem, kg)


def _mixers_kernel(pv_ref, cb_ref, cc_ref, cx_ref, xq_ref, pz_ref, cz_ref, xz_ref,
                   pvh_ref, cch_ref, cxh_ref, pw_ref, ps_ref, cw_ref, xqg_ref, km_ref, vm_ref,
                   ypool_ref, yconv_ref, ymem_ref, *, tm, tiles_per_seq):
    t0 = (pl.program_id(0) % tiles_per_seq) * tm
    has_prev = t0 > 0
    row = lax.broadcasted_iota(jnp.int32, (tm, 1), 0)

    r_i = lax.broadcasted_iota(jnp.int32, (tm, tm), 0)
    c_i = lax.broadcasted_iota(jnp.int32, (tm, tm), 1)
    rh_i = lax.broadcasted_iota(jnp.int32, (tm, HALO), 0)
    ch_i = lax.broadcasted_iota(jnp.int32, (tm, HALO), 1)
    pos = (t0 + row).astype(F32)
    for g, win in enumerate(POOL_WINDOWS):
        sl = slice(g * POOL_GW, (g + 1) * POOL_GW)
        vg = pv_ref[:, sl]
        band = ((c_i <= r_i) & (r_i - c_i < win)).astype(BF16)
        band_h = ((rh_i + (HALO - ch_i) < win) & has_prev).astype(BF16)
        wsum = _dot(band, vg) + _dot(band_h, pvh_ref[:, sl])
        cnt = jnp.minimum(pos + 1.0, float(win))
        mixed = wsum / cnt - vg.astype(F32)
        out = _dot(mixed.astype(BF16), pw_ref[g])
        ypool_ref[:, sl] = (out * ps_ref[:, sl] * pz_ref[:, sl].astype(F32)).astype(BF16)

    u = cc_ref[...].astype(F32) * cx_ref[...].astype(F32)
    uh = cch_ref[...].astype(F32) * cxh_ref[...].astype(F32)
    uh = jnp.where(has_prev, uh, 0.0)
    u1 = jnp.where(row == 0, uh[HALO - 1:HALO, :], pltpu.roll(u, 1, 0))
    u2 = jnp.where(row == 0, uh[HALO - 2:HALO - 1, :],
                   jnp.where(row == 1, uh[HALO - 1:HALO, :], pltpu.roll(u, 2, 0)))
    y = cw_ref[0:1, :] * u2 + cw_ref[1:2, :] * u1 + cw_ref[2:3, :] * u
    yconv_ref[...] = (cb_ref[...].astype(F32) * y * cz_ref[...].astype(F32)).astype(BF16)

    scale = XATTN_HEAD_DIM ** -0.5
    for hd in range(XATTN_HEADS):
        sl = slice(hd * XATTN_HEAD_DIM, (hd + 1) * XATTN_HEAD_DIM)
        qh = xq_ref[:, sl].astype(F32)
        qn = (qh * (_rms(qh, XATTN_HEAD_DIM) * scale) * xqg_ref[...]).astype(BF16)
        s = _dot_nt(qn, km_ref[:, sl])
        p = jnp.exp(s - jnp.max(s, axis=-1, keepdims=True))
        o = _dot(p.astype(BF16), vm_ref[:, sl]) / jnp.sum(p, axis=-1, keepdims=True)
        ymem_ref[:, sl] = (o * xz_ref[:, sl].astype(F32)).astype(BF16)


def _mixers(proj, pool_w, pool_scale, conv_w, xq_g, k_mem, v_mem, seq, tm):
    m = proj.shape[0]
    tiles_per_seq = seq // tm
    col = lambda c: pl.BlockSpec((tm, BRANCH_W), lambda i: (i, c))
    halo = lambda c: pl.BlockSpec((HALO, BRANCH_W), lambda i: (jnp.maximum(i * (tm // HALO) - 1, 0), c))
    full = lambda a: pl.BlockSpec(a.shape, lambda i: (0,) * a.ndim)
    memb = pl.BlockSpec((MEM_LEN, BRANCH_W), lambda i: (i // tiles_per_seq, 0))
    out = jax.ShapeDtypeStruct((m, BRANCH_W), BF16)
    return pl.pallas_call(
        functools.partial(_mixers_kernel, tm=tm, tiles_per_seq=tiles_per_seq),
        grid=(m // tm,),
        in_specs=[col(COL_PV), col(COL_CB), col(COL_CC), col(COL_CX), col(COL_XQ), col(COL_PZ), col(COL_CZ),
                  col(COL_XZ), halo(COL_PV), halo(COL_CC), halo(COL_CX),
                  full(pool_w), full(pool_scale), full(conv_w), full(xq_g), memb, memb],
        out_specs=[pl.BlockSpec((tm, BRANCH_W), lambda i: (i, 0))] * 3,
        out_shape=[out, out, out],
        compiler_params=_params("parallel"),
        name="mixers",
    )(*([proj] * 11), pool_w, pool_scale, conv_w, xq_g, k_mem, v_mem)


def _merge_kernel(y0_ref, y1_ref, y2_ref, y3_ref, g0_ref, g1_ref, g2_ref, g3_ref, wb_ref, o_ref):
    acc = None
    for b, (y_ref, g_ref) in enumerate(((y0_ref, g0_ref), (y1_ref, g1_ref), (y2_ref, g2_ref), (y3_ref, g3_ref))):
        term = g_ref[...].astype(F32) * _dot(y_ref[...], wb_ref[b])
        acc = term if acc is None else acc + term
    o_ref[...] = acc.astype(BF16)


def _merge(ys, proj, w_branch, tm, tn):
    m = proj.shape[0]
    gate0 = COL_GATE * BRANCH_W // tn
    gate = lambda b: pl.BlockSpec((tm, tn), lambda i, j: (i, gate0 + b * (D_MODEL // tn) + j))
    return pl.pallas_call(
        _merge_kernel,
        grid=(m // tm, D_MODEL // tn),
        in_specs=[pl.BlockSpec((tm, BRANCH_W), lambda i, j: (i, 0))] * N_BRANCH
        + [gate(b) for b in range(N_BRANCH)]
        + [pl.BlockSpec((N_BRANCH, BRANCH_W, tn), lambda i, j: (0, 0, j))],
        out_specs=pl.BlockSpec((tm, tn), lambda i, j: (i, j)),
        out_shape=jax.ShapeDtypeStruct((m, D_MODEL), BF16),
        compiler_params=_params("parallel", "arbitrary"),
        name="gated_merge",
    )(*ys, *([proj] * N_BRANCH), w_branch)


def _out_proj_kernel(x_ref, a_ref, w_ref, o_ref):
    o_ref[...] = x_ref[...] + _dot(a_ref[...], w_ref[...])


def _out_proj(x2, merged, w_out, tm, tn):
    m = x2.shape[0]
    return pl.pallas_call(
        _out_proj_kernel,
        grid=(m // tm, D_MODEL // tn),
        in_specs=[
            pl.BlockSpec((tm, tn), lambda i, j: (i, j)),
            pl.BlockSpec((tm, D_MODEL), lambda i, j: (i, 0)),
            pl.BlockSpec((D_MODEL, tn), lambda i, j: (0, j)),
        ],
        out_specs=pl.BlockSpec((tm, tn), lambda i, j: (i, j)),
        out_shape=jax.ShapeDtypeStruct((m, D_MODEL), F32),
        compiler_params=_params("parallel", "arbitrary"),
        name="out_proj",
    )(x2, merged, w_out)


def _layer_weights(l, norm_g, w_in, gate_b, pool_w, pool_scale, q_a_norm_g, kv_a_norm_g, w_uq, w_ukv,
                   mla_q_norm_g, mla_k_norm_g, conv_w, mem_norm_g, w_mem_kv, xattn_q_norm_g, xattn_k_norm_g,
                   w_branch, w_out):
    w = w_in[l]
    o = 0
    seg = {}
    for name, width in (("pv", BRANCH_W), ("pz", BRANCH_W), ("cq", Q_LORA), ("ckv", KV_LORA), ("kr", QK_ROPE),
                        ("mz", BRANCH_W), ("cb", BRANCH_W), ("cc", BRANCH_W), ("cx", BRANCH_W), ("cz", BRANCH_W),
                        ("xq", BRANCH_W), ("xz", BRANCH_W), ("gate", N_BRANCH * D_MODEL)):
        seg[name] = w[:, o:o + width].astype(BF16)
        o += width
    order = ("pv", "cb", "cc", "cx", "xq", "pz", "mz", "cz", "xz", "gate")
    w_main = jnp.concatenate([seg[n] for n in order], axis=1)
    b_main = jnp.concatenate([jnp.zeros(((N_PLAIN + N_SILU) * BRANCH_W,), F32), gate_b[l]])[None, :]
    w_lat = jnp.concatenate([seg["cq"], seg["ckv"], seg["kr"], jnp.zeros((D_MODEL, LANES - QK_ROPE), BF16)],
                            axis=1)
    w_uq_p = jnp.pad(w_uq[l].reshape(Q_LORA, MLA_HEADS, QK_HEAD), ((0, 0), (0, 0), (0, HEAD_PAD - QK_HEAD)))
    w_uq_p = w_uq_p.reshape(Q_LORA, MLA_HEADS * HEAD_PAD).astype(BF16)
    w_ukv3 = w_ukv[l].reshape(KV_LORA, MLA_HEADS, QK_NOPE + V_HEAD)
    w_ukv_p = jnp.concatenate([w_ukv3[:, :, :QK_NOPE].reshape(KV_LORA, MLA_HEADS * QK_NOPE),
                               w_ukv3[:, :, QK_NOPE:].reshape(KV_LORA, MLA_HEADS * V_HEAD)], axis=1).astype(BF16)
    qg_pad = jnp.pad(mla_q_norm_g[l], (0, HEAD_PAD - QK_HEAD))[None, :]
    kg_nope = mla_k_norm_g[l][None, :QK_NOPE]
    kg_tail = jnp.pad(mla_k_norm_g[l][QK_NOPE:], (0, LANES - QK_ROPE))[None, :]
    return dict(
        norm_g=norm_g[l][None, :], w_main=w_main, b_main=b_main, w_lat=w_lat,
        qag=q_a_norm_g[l][None, :], kvag=kv_a_norm_g[l][None, :], w_uq_p=w_uq_p, w_ukv_p=w_ukv_p,
        qg_pad=qg_pad, kg_nope=kg_nope, kg_tail=kg_tail,
        pool_w=pool_w[l].astype(BF16), pool_scale=pool_scale[l][None, :], conv_w=conv_w[l],
        mem_g=mem_norm_g[l][None, :], w_mem=w_mem_kv[l].astype(BF16),
        xq_g=xattn_q_norm_g[l][None, :], xk_g=xattn_k_norm_g[l][None, :],
        w_branch=w_branch[l].astype(BF16), w_out=w_out[l].astype(BF16))


def _tile(n, want):
    t = min(n, want)
    assert n % t == 0, (n, want)
    return t


def kernel(x, mem, positions, norm_g, w_in, gate_b, pool_w, pool_scale, q_a_norm_g, kv_a_norm_g, w_uq, w_ukv, mla_q_norm_g, mla_k_norm_g, conv_w, mem_norm_g, w_mem_kv, xattn_q_norm_g, xattn_k_norm_g, w_branch, w_out):
    batch, seq, _ = x.shape
    m = batch * seq
    depth = w_in.shape[0]
    assert mem.shape[1] == MEM_LEN and seq % LANES == 0
    x2 = x.reshape(m, D_MODEL)
    mem2 = mem.reshape(batch * MEM_LEN, D_MODEL)
    t_big = _tile(seq, 1024)
    t_mid = _tile(seq, 512)
    tabs = _rope_tables(positions, t_big)
    for l in range(depth):
        p = _layer_weights(l, norm_g, w_in, gate_b, pool_w, pool_scale, q_a_norm_g, kv_a_norm_g, w_uq, w_ukv,
                           mla_q_norm_g, mla_k_norm_g, conv_w, mem_norm_g, w_mem_kv, xattn_q_norm_g,
                           xattn_k_norm_g, w_branch, w_out)
        proj = _in_proj(x2, p["norm_g"], p["w_main"], p["b_main"], t_big, 1024)
        qt, k, vt = _mla_prep(x2, p["norm_g"], p["w_lat"], p["qag"], p["kvag"], p["w_uq_p"], p["w_ukv_p"],
                              p["qg_pad"], p["kg_nope"], p["kg_tail"], tabs, t_mid)
        y_mla = _attention(qt, k, vt, proj, batch, seq, t_mid, ATTN_HEADS_PER_STEP)
        k_mem, v_mem = _mem_kv(mem2, p["mem_g"], p["w_mem"], p["xk_g"])
        y_pool, y_conv, y_mem = _mixers(proj, p["pool_w"], p["pool_scale"], p["conv_w"], p["xq_g"], k_mem, v_mem,
                                        seq, t_mid)
        merged = _merge((y_pool, y_mla, y_conv, y_mem), proj, p["w_branch"], t_big, 512)
        x2 = _out_proj(x2, merged, p["w_out"], t_big, 1024)
    return x2.reshape(batch, seq, D_MODEL)
```

```python
import functools

import jax
import jax.numpy as jnp
from jax import lax
from jax.experimental import pallas as pl
from jax.experimental.pallas import tpu as pltpu

D_MODEL = 2048
MEM_LEN = 256
N_BRANCH = 4
BRANCH_W = 1024
POOL_GROUPS = 4
POOL_WINDOWS = (2, 4, 8, 16)
POOL_GW = BRANCH_W // POOL_GROUPS
MLA_HEADS = 8
Q_LORA = 512
KV_LORA = 512
QK_NOPE = 128
QK_ROPE = 64
QK_HEAD = QK_NOPE + QK_ROPE
V_HEAD = 128
ROPE_THETA = 10000.0
CONV_W = 3
XATTN_HEADS = 4
XATTN_HEAD_DIM = BRANCH_W // XATTN_HEADS
EPS = 1e-6

LANES = 128
HEAD_PAD = 2 * LANES
HALO = 16
LAT_W = Q_LORA + KV_LORA + LANES
VMEM_LIMIT = 56 * 1024 * 1024
LOG2_E = 1.4426950408889634
ATTN_HEADS_PER_STEP = 2

(COL_PV, COL_CB, COL_CC, COL_CX, COL_XQ, COL_PZ, COL_MZ, COL_CZ, COL_XZ, COL_GATE) = range(10)
N_PLAIN, N_SILU = 5, 4
N_MAIN = (N_PLAIN + N_SILU) * BRANCH_W + N_BRANCH * D_MODEL

BF16 = jnp.bfloat16
F32 = jnp.float32


def _params(*sem):
    return pltpu.CompilerParams(dimension_semantics=sem, vmem_limit_bytes=VMEM_LIMIT)


def _sigmoid(v):
    return 0.5 * jnp.tanh(0.5 * v) + 0.5


def _rms(v, n):
    return lax.rsqrt(jnp.sum(v * v, axis=-1, keepdims=True) * (1.0 / n) + EPS)


def _dot(a, b):
    return jnp.dot(a, b, preferred_element_type=F32)


def _dot_nt(a, b):
    return lax.dot_general(a, b, (((1,), (1,)), ((), ())), preferred_element_type=F32)


def _rope_table_kernel(pos_ref, inv_ref, c_ref, s1_ref, s2_ref):
    ang = pos_ref[...].astype(F32) * inv_ref[...]
    lane = lax.broadcasted_iota(jnp.int32, ang.shape, 1)
    cos, sin = jnp.cos(ang), jnp.sin(ang)
    c_ref[...] = jnp.where(lane < QK_ROPE, cos, 0.0)
    s1_ref[...] = jnp.where((lane >= QK_ROPE // 2) & (lane < QK_ROPE), sin, 0.0)
    s2_ref[...] = jnp.where(lane < QK_ROPE // 2, -sin, 0.0)


def _rope_tables(positions, tm):
    m = positions.size
    inv = ROPE_THETA ** (-jnp.arange(0, QK_ROPE, 2, dtype=F32) / QK_ROPE)
    inv_tile = jnp.concatenate([inv, inv, jnp.zeros((LANES - QK_ROPE,), F32)])[None, :]
    tab = jax.ShapeDtypeStruct((m, LANES), F32)
    return pl.pallas_call(
        _rope_table_kernel,
        grid=(m // tm,),
        in_specs=[pl.BlockSpec((tm, 1), lambda i: (i, 0)), pl.BlockSpec((1, LANES), lambda i: (0, 0))],
        out_specs=[pl.BlockSpec((tm, LANES), lambda i: (i, 0))] * 3,
        out_shape=[tab, tab, tab],
        compiler_params=_params("parallel"),
        name="rope_tables",
    )(positions.reshape(m, 1), inv_tile)


def _in_proj_kernel(x_ref, g_ref, w_ref, b_ref, o_ref, h_ref):
    j = pl.program_id(1)

    @pl.when(j == 0)
    def _():
        x = x_ref[...]
        h_ref[...] = (x * _rms(x, D_MODEL) * g_ref[...]).astype(BF16)

    @pl.when(j < N_PLAIN)
    def _():
        o_ref[...] = _dot(h_ref[...], w_ref[...]).astype(BF16)

    @pl.when((j >= N_PLAIN) & (j < N_PLAIN + N_SILU))
    def _():
        acc = _dot(h_ref[...], w_ref[...])
        o_ref[...] = (acc * _sigmoid(acc)).astype(BF16)

    @pl.when(j >= N_PLAIN + N_SILU)
    def _():
        acc = _dot(h_ref[...], w_ref[...])
        o_ref[...] = _sigmoid(acc + b_ref[...]).astype(BF16)


def _in_proj(x2, norm_g, w_main, layer, b_main, tm, tn):
    m = x2.shape[0]
    return pl.pallas_call(
        _in_proj_kernel,
        grid=(m // tm, N_MAIN // tn),
        in_specs=[
            pl.BlockSpec((tm, D_MODEL), lambda i, j: (i, 0)),
            pl.BlockSpec((1, D_MODEL), lambda i, j: (0, 0)),
            pl.BlockSpec((None, D_MODEL, tn), lambda i, j: (layer, 0, j)),
            pl.BlockSpec((1, tn), lambda i, j: (0, j)),
        ],
        out_specs=pl.BlockSpec((tm, tn), lambda i, j: (i, j)),
        out_shape=jax.ShapeDtypeStruct((m, N_MAIN), BF16),
        scratch_shapes=[pltpu.VMEM((tm, D_MODEL), BF16)],
        compiler_params=_params("parallel", "arbitrary"),
        name="in_proj",
    )(x2, norm_g, w_main, b_main)


def _rope_tile(t, c, s1, s2):
    return t * c + pltpu.roll(t, QK_ROPE // 2, 1) * s1 + pltpu.roll(t, LANES - QK_ROPE // 2, 1) * s2


def _mla_prep_kernel(x_ref, g_ref, wlat_ref, qag_ref, kvag_ref, wuq_ref, wukv_ref, qg_ref, kgn_ref, kgt_ref,
                     c_ref, s1_ref, s2_ref, qt_out, k_out, vt_out, *, n_split):
    rows_per = x_ref.shape[0] // n_split
    for part in range(n_split):
        rows = slice(part * rows_per, (part + 1) * rows_per)
        x = x_ref[rows, :]
        h = (x * _rms(x, D_MODEL) * g_ref[...]).astype(BF16)
        lat = _dot(h, wlat_ref[...])
        cq = lat[:, :Q_LORA]
        ckv = lat[:, Q_LORA:Q_LORA + KV_LORA]
        kr = lat[:, Q_LORA + KV_LORA:]
        q = _dot((cq * _rms(cq, Q_LORA) * qag_ref[...]).astype(BF16), wuq_ref[...])
        kv = _dot((ckv * _rms(ckv, KV_LORA) * kvag_ref[...]).astype(BF16), wukv_ref[...])
        c, s1, s2 = c_ref[rows, :], s1_ref[rows, :], s2_ref[rows, :]
        kr_ss = jnp.sum(kr * kr, axis=-1, keepdims=True)
        kr_rot = _rope_tile(kr * kgt_ref[...], c, s1, s2)
        qg = qg_ref[...]
        scale = QK_HEAD ** -0.5 * LOG2_E
        for hd in range(MLA_HEADS):
            qh = q[:, hd * HEAD_PAD:(hd + 1) * HEAD_PAD]
            qn = qh * (_rms(qh, QK_HEAD) * scale) * qg
            qt_out[hd * HEAD_PAD:hd * HEAD_PAD + LANES, rows] = qn[:, :LANES].T.astype(BF16)
            qt_out[hd * HEAD_PAD + LANES:(hd + 1) * HEAD_PAD, rows] = (
                _rope_tile(qn[:, LANES:], c, s1, s2).T.astype(BF16))
            kn = kv[:, hd * QK_NOPE:(hd + 1) * QK_NOPE]
            rk = lax.rsqrt((jnp.sum(kn * kn, axis=-1, keepdims=True) + kr_ss) * (1.0 / QK_HEAD) + EPS)
            k_out[rows, hd * HEAD_PAD:hd * HEAD_PAD + LANES] = (kn * rk * kgn_ref[...]).astype(BF16)
            k_out[rows, hd * HEAD_PAD + LANES:(hd + 1) * HEAD_PAD] = (kr_rot * rk).astype(BF16)
            vh = kv[:, MLA_HEADS * QK_NOPE + hd * V_HEAD:MLA_HEADS * QK_NOPE + (hd + 1) * V_HEAD]
            vt_out[hd * V_HEAD:(hd + 1) * V_HEAD, rows] = vh.T.astype(BF16)


def _mla_prep(x2, norm_g, w_lat, qag, kvag, w_uq_p, w_ukv_p, qg_pad, kg_nope, kg_tail, tabs, tm):
    m = x2.shape[0]
    row = lambda w: pl.BlockSpec((tm, w), lambda i: (i, 0))
    full = lambda a: pl.BlockSpec(a.shape, lambda i: (0,) * a.ndim)
    col = lambda h: pl.BlockSpec((h, tm), lambda i: (0, i))
    consts = (norm_g, w_lat, qag, kvag, w_uq_p, w_ukv_p, qg_pad, kg_nope, kg_tail)
    return pl.pallas_call(
        functools.partial(_mla_prep_kernel, n_split=2),
        grid=(m // tm,),
        in_specs=[row(D_MODEL)] + [full(a) for a in consts] + [row(LANES)] * 3,
        out_specs=[col(MLA_HEADS * HEAD_PAD), row(MLA_HEADS * HEAD_PAD), col(MLA_HEADS * V_HEAD)],
        out_shape=[jax.ShapeDtypeStruct((MLA_HEADS * HEAD_PAD, m), BF16),
                   jax.ShapeDtypeStruct((m, MLA_HEADS * HEAD_PAD), BF16),
                   jax.ShapeDtypeStruct((MLA_HEADS * V_HEAD, m), BF16)],
        compiler_params=_params("parallel"),
        name="mla_prep",
    )(x2, *consts, *tabs)


def _attn_kernel(qt_ref, k_ref, vt_ref, z_ref, o_ref, acc_ref, sa_ref, sb_ref, *, t, heads):
    qi = pl.program_id(2)
    acc_ref[...] = jnp.zeros(acc_ref.shape, F32)

    def scores(kj, s_ref):
        start = pl.multiple_of(kj * t, t)
        for g in range(heads):
            s_ref[g] = _dot(k_ref[pl.ds(start, t), g * HEAD_PAD:(g + 1) * HEAD_PAD],
                            qt_ref[g * HEAD_PAD:(g + 1) * HEAD_PAD, :])

    def consume(kj, s_ref, stats, masked):
        start = pl.multiple_of(kj * t, t)
        out = []
        for g in range(heads):
            m_prev, l_prev = stats[2 * g], stats[2 * g + 1]
            st = s_ref[g]
            if masked:
                kpos = kj * t + lax.broadcasted_iota(jnp.int32, (t, t), 0)
                qpos = qi * t + lax.broadcasted_iota(jnp.int32, (t, t), 1)
                st = jnp.where(kpos <= qpos, st, -jnp.inf)
            m_new = jnp.maximum(m_prev, jnp.max(st, axis=0, keepdims=True))
            alpha = jnp.exp2(m_prev - m_new)
            pt = jnp.exp2(st - m_new)
            l_new = alpha * l_prev + jnp.sum(pt, axis=0, keepdims=True)
            pv = _dot(vt_ref[g * V_HEAD:(g + 1) * V_HEAD, pl.ds(start, t)], pt.astype(BF16))
            acc_ref[g] = alpha * acc_ref[g] + pv
            out += [m_new, l_new]
        return tuple(out)

    def pair(i, stats):
        scores(2 * i + 1, sb_ref)
        stats = consume(2 * i, sa_ref, stats, False)
        scores(2 * i + 2, sa_ref)
        return consume(2 * i + 1, sb_ref, stats, False)

    init = (jnp.full((1, t), -jnp.inf, F32), jnp.zeros((1, t), F32)) * heads
    scores(0, sa_ref)
    stats = lax.fori_loop(0, qi // 2, pair, init)

    def tail_odd(stats):
        scores(qi, sb_ref)
        return consume(qi, sb_ref, consume(qi - 1, sa_ref, stats, False), True)

    stats = lax.cond(qi % 2 == 1, tail_odd, lambda stats: consume(qi, sa_ref, stats, True), stats)
    for g in range(heads):
        o = (acc_ref[g] / stats[2 * g + 1]).T
        o_ref[:, g * V_HEAD:(g + 1) * V_HEAD] = (o * z_ref[:, g * V_HEAD:(g + 1) * V_HEAD].astype(F32)).astype(BF16)


def _attention(qt, k, vt, proj, batch, seq, t, heads):
    m = k.shape[0]
    tq = t
    nq = seq // tq
    z_col = COL_MZ * BRANCH_W // (heads * V_HEAD)
    return pl.pallas_call(
        functools.partial(_attn_kernel, t=t, heads=heads),
        grid=(batch, MLA_HEADS // heads, nq),
        in_specs=[
            pl.BlockSpec((heads * HEAD_PAD, tq), lambda b, h, i: (h, b * nq + i)),
            pl.BlockSpec((seq, heads * HEAD_PAD), lambda b, h, i: (b, h)),
            pl.BlockSpec((heads * V_HEAD, seq), lambda b, h, i: (h, b)),
            pl.BlockSpec((tq, heads * V_HEAD), lambda b, h, i: (b * nq + i, z_col + h)),
        ],
        out_specs=pl.BlockSpec((tq, heads * V_HEAD), lambda b, h, i: (b * nq + i, h)),
        out_shape=jax.ShapeDtypeStruct((m, MLA_HEADS * V_HEAD), BF16),
        scratch_shapes=[pltpu.VMEM((heads, V_HEAD, t), F32),
                        pltpu.VMEM((heads, t, t), F32), pltpu.VMEM((heads, t, t), F32)],
        compiler_params=_params("parallel", "parallel", "arbitrary"),
        name="mla_attention",
    )(qt, k, vt, proj)


def _mem_kv_kernel(mem_ref, g_ref, w_ref, kg_ref, k_out, v_out):
    mm = mem_ref[...]
    kvm = _dot((mm * _rms(mm, D_MODEL) * g_ref[...]).astype(BF16), w_ref[...])
    for hd in range(XATTN_HEADS):
        kh = kvm[:, hd * XATTN_HEAD_DIM:(hd + 1) * XATTN_HEAD_DIM]
        k_out[:, hd * XATTN_HEAD_DIM:(hd + 1) * XATTN_HEAD_DIM] = (
            kh * _rms(kh, XATTN_HEAD_DIM) * kg_ref[...]).astype(BF16)
    v_out[...] = kvm[:, BRANCH_W:].astype(BF16)


def _mem_kv(mem2, mem_g, w_mem, kg):
    m = mem2.shape[0]
    out = jax.ShapeDtypeStruct((m, BRANCH_W), BF16)
    return pl.pallas_call(
        _mem_kv_kernel,
        grid=(m // MEM_LEN,),
        in_specs=[
            pl.BlockSpec((MEM_LEN, D_MODEL), lambda b: (b, 0)),
            pl.BlockSpec((1, D_MODEL), lambda b: (0, 0)),
            pl.BlockSpec((D_MODEL, 2 * BRANCH_W), lambda b: (0, 0)),
            pl.BlockSpec((1, XATTN_HEAD_DIM), lambda b: (0, 0)),
        ],
        out_specs=[pl.BlockSpec((MEM_LEN, BRANCH_W), lambda b: (b, 0))] * 2,
        out_shape=[out, out],
        compiler_params=_params("parallel"),
        name="mem_kv",
    )(mem2, mem_g, w_mem, kg)


def _mixers_kernel(pv_ref, cb_ref, cc_ref, cx_ref, xq_ref, pz_ref, cz_ref, xz_ref,
                   pvh_ref, cch_ref, cxh_ref, pw_ref, ps_ref, cw_ref, xqg_ref, km_ref, vm_ref,
                   ypool_ref, yconv_ref, ymem_ref, *, tm, tiles_per_seq):
    t0 = (pl.program_id(0) % tiles_per_seq) * tm
    has_prev = t0 > 0
    row = lax.broadcasted_iota(jnp.int32, (tm, 1), 0)

    r_i = lax.broadcasted_iota(jnp.int32, (tm, tm), 0)
    c_i = lax.broadcasted_iota(jnp.int32, (tm, tm), 1)
    rh_i = lax.broadcasted_iota(jnp.int32, (tm, HALO), 0)
    ch_i = lax.broadcasted_iota(jnp.int32, (tm, HALO), 1)
    pos = (t0 + row).astype(F32)
    for g, win in enumerate(POOL_WINDOWS):
        sl = slice(g * POOL_GW, (g + 1) * POOL_GW)
        vg = pv_ref[:, sl]
        band = ((c_i <= r_i) & (r_i - c_i < win)).astype(BF16)
        band_h = ((rh_i + (HALO - ch_i) < win) & has_prev).astype(BF16)
        wsum = _dot(band, vg) + _dot(band_h, pvh_ref[:, sl])
        cnt = jnp.minimum(pos + 1.0, float(win))
        mixed = wsum / cnt - vg.astype(F32)
        out = _dot(mixed.astype(BF16), pw_ref[g])
        ypool_ref[:, sl] = (out * ps_ref[:, sl] * pz_ref[:, sl].astype(F32)).astype(BF16)

    u = cc_ref[...].astype(F32) * cx_ref[...].astype(F32)
    uh = cch_ref[...].astype(F32) * cxh_ref[...].astype(F32)
    uh = jnp.where(has_prev, uh, 0.0)
    u1 = jnp.where(row == 0, uh[HALO - 1:HALO, :], pltpu.roll(u, 1, 0))
    u2 = jnp.where(row == 0, uh[HALO - 2:HALO - 1, :],
                   jnp.where(row == 1, uh[HALO - 1:HALO, :], pltpu.roll(u, 2, 0)))
    y = cw_ref[0:1, :] * u2 + cw_ref[1:2, :] * u1 + cw_ref[2:3, :] * u
    yconv_ref[...] = (cb_ref[...].astype(F32) * y * cz_ref[...].astype(F32)).astype(BF16)

    scale = XATTN_HEAD_DIM ** -0.5
    for hd in range(XATTN_HEADS):
        sl = slice(hd * XATTN_HEAD_DIM, (hd + 1) * XATTN_HEAD_DIM)
        qh = xq_ref[:, sl].astype(F32)
        qn = (qh * (_rms(qh, XATTN_HEAD_DIM) * scale) * xqg_ref[...]).astype(BF16)
        s = _dot_nt(qn, km_ref[:, sl])
        p = jnp.exp(s - jnp.max(s, axis=-1, keepdims=True))
        o = _dot(p.astype(BF16), vm_ref[:, sl]) / jnp.sum(p, axis=-1, keepdims=True)
        ymem_ref[:, sl] = (o * xz_ref[:, sl].astype(F32)).astype(BF16)


def _mixers(proj, pool_w, pool_scale, conv_w, xq_g, k_mem, v_mem, seq, tm):
    m = proj.shape[0]
    tiles_per_seq = seq // tm
    col = lambda c: pl.BlockSpec((tm, BRANCH_W), lambda i: (i, c))
    halo = lambda c: pl.BlockSpec((HALO, BRANCH_W), lambda i: (jnp.maximum(i * (tm // HALO) - 1, 0), c))
    full = lambda a: pl.BlockSpec(a.shape, lambda i: (0,) * a.ndim)
    memb = pl.BlockSpec((MEM_LEN, BRANCH_W), lambda i: (i // tiles_per_seq, 0))
    out = jax.ShapeDtypeStruct((m, BRANCH_W), BF16)
    return pl.pallas_call(
        functools.partial(_mixers_kernel, tm=tm, tiles_per_seq=tiles_per_seq),
        grid=(m // tm,),
        in_specs=[col(COL_PV), col(COL_CB), col(COL_CC), col(COL_CX), col(COL_XQ), col(COL_PZ), col(COL_CZ),
                  col(COL_XZ), halo(COL_PV), halo(COL_CC), halo(COL_CX),
                  full(pool_w), full(pool_scale), full(conv_w), full(xq_g), memb, memb],
        out_specs=[pl.BlockSpec((tm, BRANCH_W), lambda i: (i, 0))] * 3,
        out_shape=[out, out, out],
        compiler_params=_params("parallel"),
        name="mixers",
    )(*([proj] * 11), pool_w, pool_scale, conv_w, xq_g, k_mem, v_mem)


def _merge_kernel(y0_ref, y1_ref, y2_ref, y3_ref, g0_ref, g1_ref, g2_ref, g3_ref, wb_ref, o_ref):
    acc = None
    for b, (y_ref, g_ref) in enumerate(((y0_ref, g0_ref), (y1_ref, g1_ref), (y2_ref, g2_ref), (y3_ref, g3_ref))):
        term = g_ref[...].astype(F32) * _dot(y_ref[...], wb_ref[b])
        acc = term if acc is None else acc + term
    o_ref[...] = acc.astype(BF16)


def _merge(ys, proj, w_branch, tm, tn):
    m = proj.shape[0]
    gate0 = COL_GATE * BRANCH_W // tn
    gate = lambda b: pl.BlockSpec((tm, tn), lambda i, j: (i, gate0 + b * (D_MODEL // tn) + j))
    return pl.pallas_call(
        _merge_kernel,
        grid=(m // tm, D_MODEL // tn),
        in_specs=[pl.BlockSpec((tm, BRANCH_W), lambda i, j: (i, 0))] * N_BRANCH
        + [gate(b) for b in range(N_BRANCH)]
        + [pl.BlockSpec((N_BRANCH, BRANCH_W, tn), lambda i, j: (0, 0, j))],
        out_specs=pl.BlockSpec((tm, tn), lambda i, j: (i, j)),
        out_shape=jax.ShapeDtypeStruct((m, D_MODEL), BF16),
        compiler_params=_params("parallel", "arbitrary"),
        name="gated_merge",
    )(*ys, *([proj] * N_BRANCH), w_branch)


def _out_proj_kernel(x_ref, a_ref, w_ref, o_ref):
    o_ref[...] = x_ref[...] + _dot(a_ref[...], w_ref[...])


def _out_proj(x2, merged, w_out, tm, tn):
    m = x2.shape[0]
    return pl.pallas_call(
        _out_proj_kernel,
        grid=(m // tm, D_MODEL // tn),
        in_specs=[
            pl.BlockSpec((tm, tn), lambda i, j: (i, j)),
            pl.BlockSpec((tm, D_MODEL), lambda i, j: (i, 0)),
            pl.BlockSpec((D_MODEL, tn), lambda i, j: (0, j)),
        ],
        out_specs=pl.BlockSpec((tm, tn), lambda i, j: (i, j)),
        out_shape=jax.ShapeDtypeStruct((m, D_MODEL), F32),
        compiler_params=_params("parallel", "arbitrary"),
        name="out_proj",
    )(x2, merged, w_out)


MLA_COLS = Q_LORA + KV_LORA + QK_ROPE
SHIFTED_COL0 = 2 * BRANCH_W + MLA_COLS
LANE_SHIFT = SHIFTED_COL0 % LANES
SHIFTED_ORDER = (COL_MZ, COL_CB, COL_CC, COL_CX, COL_CZ, COL_XQ, COL_XZ) + tuple(
    COL_GATE + g for g in range(N_BRANCH * D_MODEL // BRANCH_W))


def _w_relayout_kernel(mb_ref, eb_ref, main_ref, extra_ref, o_ref):
    del mb_ref, eb_ref
    j = pl.program_id(2)
    aligned = (j == COL_PV) | (j == COL_PZ)

    @pl.when(aligned)
    def _():
        o_ref[...] = main_ref[...].astype(BF16)

    @pl.when(jnp.logical_not(aligned))
    def _():
        cat = jnp.concatenate([main_ref[...], extra_ref[...]], axis=1)
        o_ref[...] = cat[:, LANE_SHIFT:LANE_SHIFT + BRANCH_W].astype(BF16)


def _w_relayout(w_in, tr):
    depth = w_in.shape[0]
    n_tiles = N_MAIN // BRANCH_W
    base = (SHIFTED_COL0 - LANE_SHIFT) // BRANCH_W
    main_blk, extra_blk = [0] * n_tiles, [0] * n_tiles
    main_blk[COL_PV], main_blk[COL_PZ] = 0, 1
    for r, slot in enumerate(SHIFTED_ORDER):
        main_blk[slot] = base + r
        extra_blk[slot] = (base + r + 1) * (BRANCH_W // LANES)
    grid_spec = pltpu.PrefetchScalarGridSpec(
        num_scalar_prefetch=2,
        grid=(depth, D_MODEL // tr, n_tiles),
        in_specs=[
            pl.BlockSpec((None, tr, BRANCH_W), lambda l, r, j, mb, eb: (l, r, mb[j])),
            pl.BlockSpec((None, tr, LANES), lambda l, r, j, mb, eb: (l, r, eb[j])),
        ],
        out_specs=pl.BlockSpec((None, tr, BRANCH_W), lambda l, r, j, mb, eb: (l, r, j)),
    )
    return pl.pallas_call(
        _w_relayout_kernel,
        grid_spec=grid_spec,
        out_shape=jax.ShapeDtypeStruct((depth, D_MODEL, N_MAIN), BF16),
        compiler_params=_params("parallel", "parallel", "arbitrary"),
        name="w_in_relayout",
    )(jnp.asarray(main_blk, jnp.int32), jnp.asarray(extra_blk, jnp.int32), w_in, w_in)


def _layer_weights(l, norm_g, w_in, gate_b, pool_w, pool_scale, q_a_norm_g, kv_a_norm_g, w_uq, w_ukv,
                   mla_q_norm_g, mla_k_norm_g, conv_w, mem_norm_g, w_mem_kv, xattn_q_norm_g, xattn_k_norm_g,
                   w_branch, w_out):
    w = w_in[l]
    o = 0
    seg = {}
    for name, width in (("pv", BRANCH_W), ("pz", BRANCH_W), ("cq", Q_LORA), ("ckv", KV_LORA), ("kr", QK_ROPE),
                        ("mz", BRANCH_W), ("cb", BRANCH_W), ("cc", BRANCH_W), ("cx", BRANCH_W), ("cz", BRANCH_W),
                        ("xq", BRANCH_W), ("xz", BRANCH_W), ("gate", N_BRANCH * D_MODEL)):
        if name in ("cq", "ckv", "kr"):
            seg[name] = w[:, o:o + width].astype(BF16)
        o += width
    b_main = jnp.concatenate([jnp.zeros(((N_PLAIN + N_SILU) * BRANCH_W,), F32), gate_b[l]])[None, :]
    w_lat = jnp.concatenate([seg["cq"], seg["ckv"], seg["kr"], jnp.zeros((D_MODEL, LANES - QK_ROPE), BF16)],
                            axis=1)
    w_uq_p = jnp.pad(w_uq[l].reshape(Q_LORA, MLA_HEADS, QK_HEAD), ((0, 0), (0, 0), (0, HEAD_PAD - QK_HEAD)))
    w_uq_p = w_uq_p.reshape(Q_LORA, MLA_HEADS * HEAD_PAD).astype(BF16)
    w_ukv3 = w_ukv[l].reshape(KV_LORA, MLA_HEADS, QK_NOPE + V_HEAD)
    w_ukv_p = jnp.concatenate([w_ukv3[:, :, :QK_NOPE].reshape(KV_LORA, MLA_HEADS * QK_NOPE),
                               w_ukv3[:, :, QK_NOPE:].reshape(KV_LORA, MLA_HEADS * V_HEAD)], axis=1).astype(BF16)
    qg_pad = jnp.pad(mla_q_norm_g[l], (0, HEAD_PAD - QK_HEAD))[None, :]
    kg_nope = mla_k_norm_g[l][None, :QK_NOPE]
    kg_tail = jnp.pad(mla_k_norm_g[l][QK_NOPE:], (0, LANES - QK_ROPE))[None, :]
    return dict(
        norm_g=norm_g[l][None, :], b_main=b_main, w_lat=w_lat,
        qag=q_a_norm_g[l][None, :], kvag=kv_a_norm_g[l][None, :], w_uq_p=w_uq_p, w_ukv_p=w_ukv_p,
        qg_pad=qg_pad, kg_nope=kg_nope, kg_tail=kg_tail,
        pool_w=pool_w[l].astype(BF16), pool_scale=pool_scale[l][None, :], conv_w=conv_w[l],
        mem_g=mem_norm_g[l][None, :], w_mem=w_mem_kv[l].astype(BF16),
        xq_g=xattn_q_norm_g[l][None, :], xk_g=xattn_k_norm_g[l][None, :],
        w_branch=w_branch[l].astype(BF16), w_out=w_out[l].astype(BF16))


def _tile(n, want):
    t = min(n, want)
    assert n % t == 0, (n, want)
    return t


def kernel(x, mem, positions, norm_g, w_in, gate_b, pool_w, pool_scale, q_a_norm_g, kv_a_norm_g, w_uq, w_ukv, mla_q_norm_g, mla_k_norm_g, conv_w, mem_norm_g, w_mem_kv, xattn_q_norm_g, xattn_k_norm_g, w_branch, w_out):
    batch, seq, _ = x.shape
    m = batch * seq
    depth = w_in.shape[0]
    assert mem.shape[1] == MEM_LEN and seq % LANES == 0
    x2 = x.reshape(m, D_MODEL)
    mem2 = mem.reshape(batch * MEM_LEN, D_MODEL)
    t_big = _tile(seq, 1024)
    t_mid = _tile(seq, 512)
    tabs = _rope_tables(positions, t_big)
    w_main_all = _w_relayout(w_in, 1024)
    for l in range(depth):
        p = _layer_weights(l, norm_g, w_in, gate_b, pool_w, pool_scale, q_a_norm_g, kv_a_norm_g, w_uq, w_ukv,
                           mla_q_norm_g, mla_k_norm_g, conv_w, mem_norm_g, w_mem_kv, xattn_q_norm_g,
                           xattn_k_norm_g, w_branch, w_out)
        proj = _in_proj(x2, p["norm_g"], w_main_all, l, p["b_main"], t_big, BRANCH_W)
        qt, k, vt = _mla_prep(x2, p["norm_g"], p["w_lat"], p["qag"], p["kvag"], p["w_uq_p"], p["w_ukv_p"],
                              p["qg_pad"], p["kg_nope"], p["kg_tail"], tabs, t_mid)
        y_mla = _attention(qt, k, vt, proj, batch, seq, t_mid, ATTN_HEADS_PER_STEP)
        k_mem, v_mem = _mem_kv(mem2, p["mem_g"], p["w_mem"], p["xk_g"])
        y_pool, y_conv, y_mem = _mixers(proj, p["pool_w"], p["pool_scale"], p["conv_w"], p["xq_g"], k_mem, v_mem,
                                        seq, t_mid)
        merged = _merge((y_pool, y_mla, y_conv, y_mem), proj, p["w_branch"], t_big, 512)
        x2 = _out_proj(x2, merged, p["w_out"], t_big, 1024)
    return x2.reshape(batch, seq, D_MODEL)
```

```python
import functools

import jax
import jax.numpy as jnp
from jax import lax
from jax.experimental import pallas as pl
from jax.experimental.pallas import tpu as pltpu

D_MODEL = 2048
MEM_LEN = 256
N_BRANCH = 4
BRANCH_W = 1024
POOL_GROUPS = 4
POOL_WINDOWS = (2, 4, 8, 16)
POOL_GW = BRANCH_W // POOL_GROUPS
MLA_HEADS = 8
Q_LORA = 512
KV_LORA = 512
QK_NOPE = 128
QK_ROPE = 64
QK_HEAD = QK_NOPE + QK_ROPE
V_HEAD = 128
ROPE_THETA = 10000.0
CONV_W = 3
XATTN_HEADS = 4
XATTN_HEAD_DIM = BRANCH_W // XATTN_HEADS
EPS = 1e-6

LANES = 128
HEAD_PAD = 2 * LANES
HALO = 16
LAT_W = Q_LORA + KV_LORA + LANES
VMEM_LIMIT = 56 * 1024 * 1024
LOG2_E = 1.4426950408889634
ATTN_HEADS_PER_STEP = 2

(COL_PV, COL_CB, COL_CC, COL_CX, COL_XQ, COL_PZ, COL_MZ, COL_CZ, COL_XZ, COL_GATE) = range(10)
N_PLAIN, N_SILU = 5, 4
N_MAIN = (N_PLAIN + N_SILU) * BRANCH_W + N_BRANCH * D_MODEL

BF16 = jnp.bfloat16
F32 = jnp.float32


def _params(*sem):
    return pltpu.CompilerParams(dimension_semantics=sem, vmem_limit_bytes=VMEM_LIMIT)


def _sigmoid(v):
    return 0.5 * jnp.tanh(0.5 * v) + 0.5


def _rms(v, n):
    return lax.rsqrt(jnp.sum(v * v, axis=-1, keepdims=True) * (1.0 / n) + EPS)


def _dot(a, b):
    return jnp.dot(a, b, preferred_element_type=F32)


def _dot_nt(a, b):
    return lax.dot_general(a, b, (((1,), (1,)), ((), ())), preferred_element_type=F32)


def _rope_table_kernel(pos_ref, inv_ref, c_ref, s1_ref, s2_ref):
    ang = pos_ref[...].astype(F32) * inv_ref[...]
    lane = lax.broadcasted_iota(jnp.int32, ang.shape, 1)
    cos, sin = jnp.cos(ang), jnp.sin(ang)
    c_ref[...] = jnp.where(lane < QK_ROPE, cos, 0.0)
    s1_ref[...] = jnp.where((lane >= QK_ROPE // 2) & (lane < QK_ROPE), sin, 0.0)
    s2_ref[...] = jnp.where(lane < QK_ROPE // 2, -sin, 0.0)


def _rope_tables(positions, tm):
    m = positions.size
    inv = ROPE_THETA ** (-jnp.arange(0, QK_ROPE, 2, dtype=F32) / QK_ROPE)
    inv_tile = jnp.concatenate([inv, inv, jnp.zeros((LANES - QK_ROPE,), F32)])[None, :]
    tab = jax.ShapeDtypeStruct((m, LANES), F32)
    return pl.pallas_call(
        _rope_table_kernel,
        grid=(m // tm,),
        in_specs=[pl.BlockSpec((tm, 1), lambda i: (i, 0)), pl.BlockSpec((1, LANES), lambda i: (0, 0))],
        out_specs=[pl.BlockSpec((tm, LANES), lambda i: (i, 0))] * 3,
        out_shape=[tab, tab, tab],
        compiler_params=_params("parallel"),
        name="rope_tables",
    )(positions.reshape(m, 1), inv_tile)


def _in_proj_kernel(x_ref, g_ref, w_ref, b_ref, o_ref, h_ref):
    j = pl.program_id(1)

    @pl.when(j == 0)
    def _():
        x = x_ref[...]
        h_ref[...] = (x * _rms(x, D_MODEL) * g_ref[...]).astype(BF16)

    @pl.when(j < N_PLAIN)
    def _():
        o_ref[...] = _dot(h_ref[...], w_ref[...]).astype(BF16)

    @pl.when((j >= N_PLAIN) & (j < N_PLAIN + N_SILU))
    def _():
        acc = _dot(h_ref[...], w_ref[...])
        o_ref[...] = (acc * _sigmoid(acc)).astype(BF16)

    @pl.when(j >= N_PLAIN + N_SILU)
    def _():
        acc = _dot(h_ref[...], w_ref[...])
        o_ref[...] = _sigmoid(acc + b_ref[...]).astype(BF16)


def _in_proj(x2, norm_g, w_main, layer, b_main, tm, tn):
    m = x2.shape[0]
    return pl.pallas_call(
        _in_proj_kernel,
        grid=(m // tm, N_MAIN // tn),
        in_specs=[
            pl.BlockSpec((tm, D_MODEL), lambda i, j: (i, 0)),
            pl.BlockSpec((1, D_MODEL), lambda i, j: (0, 0)),
            pl.BlockSpec((None, D_MODEL, tn), lambda i, j: (layer, 0, j)),
            pl.BlockSpec((1, tn), lambda i, j: (0, j)),
        ],
        out_specs=pl.BlockSpec((tm, tn), lambda i, j: (i, j)),
        out_shape=jax.ShapeDtypeStruct((m, N_MAIN), BF16),
        scratch_shapes=[pltpu.VMEM((tm, D_MODEL), BF16)],
        compiler_params=_params("parallel", "arbitrary"),
        name="in_proj",
    )(x2, norm_g, w_main, b_main)


def _rope_tile(t, c, s1, s2):
    return t * c + pltpu.roll(t, QK_ROPE // 2, 1) * s1 + pltpu.roll(t, LANES - QK_ROPE // 2, 1) * s2


def _mla_prep_kernel(x_ref, g_ref, wlat_ref, qag_ref, kvag_ref, wuq_ref, wukv_ref, qg_ref, kgn_ref, kgt_ref,
                     c_ref, s1_ref, s2_ref, qt_out, k_out, vt_out, *, n_split):
    rows_per = x_ref.shape[0] // n_split
    for part in range(n_split):
        rows = slice(part * rows_per, (part + 1) * rows_per)
        x = x_ref[rows, :]
        h = (x * _rms(x, D_MODEL) * g_ref[...]).astype(BF16)
        lat = _dot(h, wlat_ref[...])
        cq = lat[:, :Q_LORA]
        ckv = lat[:, Q_LORA:Q_LORA + KV_LORA]
        kr = lat[:, Q_LORA + KV_LORA:]
        q = _dot((cq * _rms(cq, Q_LORA) * qag_ref[...]).astype(BF16), wuq_ref[...])
        kv = _dot((ckv * _rms(ckv, KV_LORA) * kvag_ref[...]).astype(BF16), wukv_ref[...])
        c, s1, s2 = c_ref[rows, :], s1_ref[rows, :], s2_ref[rows, :]
        kr_ss = jnp.sum(kr * kr, axis=-1, keepdims=True)
        kr_rot = _rope_tile(kr * kgt_ref[...], c, s1, s2)
        qg = qg_ref[...]
        scale = QK_HEAD ** -0.5 * LOG2_E
        for hd in range(MLA_HEADS):
            qh = q[:, hd * HEAD_PAD:(hd + 1) * HEAD_PAD]
            qn = qh * (_rms(qh, QK_HEAD) * scale) * qg
            qt_out[hd * HEAD_PAD:hd * HEAD_PAD + LANES, rows] = qn[:, :LANES].T.astype(BF16)
            qt_out[hd * HEAD_PAD + LANES:(hd + 1) * HEAD_PAD, rows] = (
                _rope_tile(qn[:, LANES:], c, s1, s2).T.astype(BF16))
            kn = kv[:, hd * QK_NOPE:(hd + 1) * QK_NOPE]
            rk = lax.rsqrt((jnp.sum(kn * kn, axis=-1, keepdims=True) + kr_ss) * (1.0 / QK_HEAD) + EPS)
            k_out[rows, hd * HEAD_PAD:hd * HEAD_PAD + LANES] = (kn * rk * kgn_ref[...]).astype(BF16)
            k_out[rows, hd * HEAD_PAD + LANES:(hd + 1) * HEAD_PAD] = (kr_rot * rk).astype(BF16)
            vh = kv[:, MLA_HEADS * QK_NOPE + hd * V_HEAD:MLA_HEADS * QK_NOPE + (hd + 1) * V_HEAD]
            vt_out[hd * V_HEAD:(hd + 1) * V_HEAD, rows] = vh.T.astype(BF16)


def _mla_prep(x2, norm_g, w_lat, layer, qag, kvag, w_uq_p, w_ukv_p, qg_pad, kg_nope, kg_tail, tabs, tm):
    m = x2.shape[0]
    row = lambda w: pl.BlockSpec((tm, w), lambda i: (i, 0))
    full = lambda a: pl.BlockSpec(a.shape, lambda i: (0,) * a.ndim)
    col = lambda h: pl.BlockSpec((h, tm), lambda i: (0, i))
    lat = pl.BlockSpec((None, D_MODEL, LAT_W), lambda i: (layer, 0, 0))
    consts = (norm_g, w_lat, qag, kvag, w_uq_p, w_ukv_p, qg_pad, kg_nope, kg_tail)
    return pl.pallas_call(
        functools.partial(_mla_prep_kernel, n_split=2),
        grid=(m // tm,),
        in_specs=[row(D_MODEL)] + [lat if a is w_lat else full(a) for a in consts] + [row(LANES)] * 3,
        out_specs=[col(MLA_HEADS * HEAD_PAD), row(MLA_HEADS * HEAD_PAD), col(MLA_HEADS * V_HEAD)],
        out_shape=[jax.ShapeDtypeStruct((MLA_HEADS * HEAD_PAD, m), BF16),
                   jax.ShapeDtypeStruct((m, MLA_HEADS * HEAD_PAD), BF16),
                   jax.ShapeDtypeStruct((MLA_HEADS * V_HEAD, m), BF16)],
        compiler_params=_params("parallel"),
        name="mla_prep",
    )(x2, *consts, *tabs)


def _attn_kernel(qt_ref, k_ref, vt_ref, z_ref, o_ref, acc_ref, sa_ref, sb_ref, *, t, heads):
    qi = pl.program_id(2)
    acc_ref[...] = jnp.zeros(acc_ref.shape, F32)

    def scores(kj, s_ref):
        start = pl.multiple_of(kj * t, t)
        for g in range(heads):
            s_ref[g] = _dot(k_ref[pl.ds(start, t), g * HEAD_PAD:(g + 1) * HEAD_PAD],
                            qt_ref[g * HEAD_PAD:(g + 1) * HEAD_PAD, :])

    def consume(kj, s_ref, stats, masked):
        start = pl.multiple_of(kj * t, t)
        out = []
        for g in range(heads):
            m_prev, l_prev = stats[2 * g], stats[2 * g + 1]
            st = s_ref[g]
            if masked:
                kpos = kj * t + lax.broadcasted_iota(jnp.int32, (t, t), 0)
                qpos = qi * t + lax.broadcasted_iota(jnp.int32, (t, t), 1)
                st = jnp.where(kpos <= qpos, st, -jnp.inf)
            m_new = jnp.maximum(m_prev, jnp.max(st, axis=0, keepdims=True))
            alpha = jnp.exp2(m_prev - m_new)
            pt = jnp.exp2(st - m_new)
            l_new = alpha * l_prev + jnp.sum(pt, axis=0, keepdims=True)
            pv = _dot(vt_ref[g * V_HEAD:(g + 1) * V_HEAD, pl.ds(start, t)], pt.astype(BF16))
            acc_ref[g] = alpha * acc_ref[g] + pv
            out += [m_new, l_new]
        return tuple(out)

    def pair(i, stats):
        scores(2 * i + 1, sb_ref)
        stats = consume(2 * i, sa_ref, stats, False)
        scores(2 * i + 2, sa_ref)
        return consume(2 * i + 1, sb_ref, stats, False)

    init = (jnp.full((1, t), -jnp.inf, F32), jnp.zeros((1, t), F32)) * heads
    scores(0, sa_ref)
    stats = lax.fori_loop(0, qi // 2, pair, init)

    def tail_odd(stats):
        scores(qi, sb_ref)
        return consume(qi, sb_ref, consume(qi - 1, sa_ref, stats, False), True)

    stats = lax.cond(qi % 2 == 1, tail_odd, lambda stats: consume(qi, sa_ref, stats, True), stats)
    for g in range(heads):
        o = (acc_ref[g] / stats[2 * g + 1]).T
        o_ref[:, g * V_HEAD:(g + 1) * V_HEAD] = (o * z_ref[:, g * V_HEAD:(g + 1) * V_HEAD].astype(F32)).astype(BF16)


def _attention(qt, k, vt, proj, batch, seq, t, heads):
    m = k.shape[0]
    tq = t
    nq = seq // tq
    z_col = COL_MZ * BRANCH_W // (heads * V_HEAD)
    return pl.pallas_call(
        functools.partial(_attn_kernel, t=t, heads=heads),
        grid=(batch, MLA_HEADS // heads, nq),
        in_specs=[
            pl.BlockSpec((heads * HEAD_PAD, tq), lambda b, h, i: (h, b * nq + i)),
            pl.BlockSpec((seq, heads * HEAD_PAD), lambda b, h, i: (b, h)),
            pl.BlockSpec((heads * V_HEAD, seq), lambda b, h, i: (h, b)),
            pl.BlockSpec((tq, heads * V_HEAD), lambda b, h, i: (b * nq + i, z_col + h)),
        ],
        out_specs=pl.BlockSpec((tq, heads * V_HEAD), lambda b, h, i: (b * nq + i, h)),
        out_shape=jax.ShapeDtypeStruct((m, MLA_HEADS * V_HEAD), BF16),
        scratch_shapes=[pltpu.VMEM((heads, V_HEAD, t), F32),
                        pltpu.VMEM((heads, t, t), F32), pltpu.VMEM((heads, t, t), F32)],
        compiler_params=_params("parallel", "parallel", "arbitrary"),
        name="mla_attention",
    )(qt, k, vt, proj)


def _mem_kv_kernel(mem_ref, g_ref, w_ref, kg_ref, k_out, v_out):
    mm = mem_ref[...]
    kvm = _dot((mm * _rms(mm, D_MODEL) * g_ref[...]).astype(BF16), w_ref[...])
    for hd in range(XATTN_HEADS):
        kh = kvm[:, hd * XATTN_HEAD_DIM:(hd + 1) * XATTN_HEAD_DIM]
        k_out[:, hd * XATTN_HEAD_DIM:(hd + 1) * XATTN_HEAD_DIM] = (
            kh * _rms(kh, XATTN_HEAD_DIM) * kg_ref[...]).astype(BF16)
    v_out[...] = kvm[:, BRANCH_W:].astype(BF16)


def _mem_kv(mem2, mem_g, w_mem, kg):
    m = mem2.shape[0]
    out = jax.ShapeDtypeStruct((m, BRANCH_W), BF16)
    return pl.pallas_call(
        _mem_kv_kernel,
        grid=(m // MEM_LEN,),
        in_specs=[
            pl.BlockSpec((MEM_LEN, D_MODEL), lambda b: (b, 0)),
            pl.BlockSpec((1, D_MODEL), lambda b: (0, 0)),
            pl.BlockSpec((D_MODEL, 2 * BRANCH_W), lambda b: (0, 0)),
            pl.BlockSpec((1, XATTN_HEAD_DIM), lambda b: (0, 0)),
        ],
        out_specs=[pl.BlockSpec((MEM_LEN, BRANCH_W), lambda b: (b, 0))] * 2,
        out_shape=[out, out],
        compiler_params=_params("parallel"),
        name="mem_kv",
    )(mem2, mem_g, w_mem, kg)


def _mixers_kernel(pv_ref, cb_ref, cc_ref, cx_ref, xq_ref, pz_ref, cz_ref, xz_ref,
                   pvh_ref, cch_ref, cxh_ref, pw_ref, ps_ref, cw_ref, xqg_ref, km_ref, vm_ref,
                   ypool_ref, yconv_ref, ymem_ref, *, tm, tiles_per_seq):
    t0 = (pl.program_id(0) % tiles_per_seq) * tm
    has_prev = t0 > 0
    row = lax.broadcasted_iota(jnp.int32, (tm, 1), 0)

    r_i = lax.broadcasted_iota(jnp.int32, (tm, tm), 0)
    c_i = lax.broadcasted_iota(jnp.int32, (tm, tm), 1)
    rh_i = lax.broadcasted_iota(jnp.int32, (tm, HALO), 0)
    ch_i = lax.broadcasted_iota(jnp.int32, (tm, HALO), 1)
    pos = (t0 + row).astype(F32)
    for g, win in enumerate(POOL_WINDOWS):
        sl = slice(g * POOL_GW, (g + 1) * POOL_GW)
        vg = pv_ref[:, sl]
        band = ((c_i <= r_i) & (r_i - c_i < win)).astype(BF16)
        band_h = ((rh_i + (HALO - ch_i) < win) & has_prev).astype(BF16)
        wsum = _dot(band, vg) + _dot(band_h, pvh_ref[:, sl])
        cnt = jnp.minimum(pos + 1.0, float(win))
        mixed = wsum / cnt - vg.astype(F32)
        out = _dot(mixed.astype(BF16), pw_ref[g])
        ypool_ref[:, sl] = (out * ps_ref[:, sl] * pz_ref[:, sl].astype(F32)).astype(BF16)

    u = cc_ref[...].astype(F32) * cx_ref[...].astype(F32)
    uh = cch_ref[...].astype(F32) * cxh_ref[...].astype(F32)
    uh = jnp.where(has_prev, uh, 0.0)
    u1 = jnp.where(row == 0, uh[HALO - 1:HALO, :], pltpu.roll(u, 1, 0))
    u2 = jnp.where(row == 0, uh[HALO - 2:HALO - 1, :],
                   jnp.where(row == 1, uh[HALO - 1:HALO, :], pltpu.roll(u, 2, 0)))
    y = cw_ref[0:1, :] * u2 + cw_ref[1:2, :] * u1 + cw_ref[2:3, :] * u
    yconv_ref[...] = (cb_ref[...].astype(F32) * y * cz_ref[...].astype(F32)).astype(BF16)

    scale = XATTN_HEAD_DIM ** -0.5
    for hd in range(XATTN_HEADS):
        sl = slice(hd * XATTN_HEAD_DIM, (hd + 1) * XATTN_HEAD_DIM)
        qh = xq_ref[:, sl].astype(F32)
        qn = (qh * (_rms(qh, XATTN_HEAD_DIM) * scale) * xqg_ref[...]).astype(BF16)
        s = _dot_nt(qn, km_ref[:, sl])
        p = jnp.exp(s - jnp.max(s, axis=-1, keepdims=True))
        o = _dot(p.astype(BF16), vm_ref[:, sl]) / jnp.sum(p, axis=-1, keepdims=True)
        ymem_ref[:, sl] = (o * xz_ref[:, sl].astype(F32)).astype(BF16)


def _mixers(proj, pool_w, pool_scale, conv_w, xq_g, k_mem, v_mem, seq, tm):
    m = proj.shape[0]
    tiles_per_seq = seq // tm
    col = lambda c: pl.BlockSpec((tm, BRANCH_W), lambda i: (i, c))
    halo = lambda c: pl.BlockSpec((HALO, BRANCH_W), lambda i: (jnp.maximum(i * (tm // HALO) - 1, 0), c))
    full = lambda a: pl.BlockSpec(a.shape, lambda i: (0,) * a.ndim)
    memb = pl.BlockSpec((MEM_LEN, BRANCH_W), lambda i: (i // tiles_per_seq, 0))
    out = jax.ShapeDtypeStruct((m, BRANCH_W), BF16)
    return pl.pallas_call(
        functools.partial(_mixers_kernel, tm=tm, tiles_per_seq=tiles_per_seq),
        grid=(m // tm,),
        in_specs=[col(COL_PV), col(COL_CB), col(COL_CC), col(COL_CX), col(COL_XQ), col(COL_PZ), col(COL_CZ),
                  col(COL_XZ), halo(COL_PV), halo(COL_CC), halo(COL_CX),
                  full(pool_w), full(pool_scale), full(conv_w), full(xq_g), memb, memb],
        out_specs=[pl.BlockSpec((tm, BRANCH_W), lambda i: (i, 0))] * 3,
        out_shape=[out, out, out],
        compiler_params=_params("parallel"),
        name="mixers",
    )(*([proj] * 11), pool_w, pool_scale, conv_w, xq_g, k_mem, v_mem)


def _merge_kernel(y0_ref, y1_ref, y2_ref, y3_ref, g0_ref, g1_ref, g2_ref, g3_ref, wb_ref, o_ref):
    acc = None
    for b, (y_ref, g_ref) in enumerate(((y0_ref, g0_ref), (y1_ref, g1_ref), (y2_ref, g2_ref), (y3_ref, g3_ref))):
        term = g_ref[...].astype(F32) * _dot(y_ref[...], wb_ref[b])
        acc = term if acc is None else acc + term
    o_ref[...] = acc.astype(BF16)


def _merge(ys, proj, w_branch, tm, tn):
    m = proj.shape[0]
    gate0 = COL_GATE * BRANCH_W // tn
    gate = lambda b: pl.BlockSpec((tm, tn), lambda i, j: (i, gate0 + b * (D_MODEL // tn) + j))
    return pl.pallas_call(
        _merge_kernel,
        grid=(m // tm, D_MODEL // tn),
        in_specs=[pl.BlockSpec((tm, BRANCH_W), lambda i, j: (i, 0))] * N_BRANCH
        + [gate(b) for b in range(N_BRANCH)]
        + [pl.BlockSpec((N_BRANCH, BRANCH_W, tn), lambda i, j: (0, 0, j))],
        out_specs=pl.BlockSpec((tm, tn), lambda i, j: (i, j)),
        out_shape=jax.ShapeDtypeStruct((m, D_MODEL), BF16),
        compiler_params=_params("parallel", "arbitrary"),
        name="gated_merge",
    )(*ys, *([proj] * N_BRANCH), w_branch)


def _out_proj_kernel(x_ref, a_ref, w_ref, o_ref):
    o_ref[...] = x_ref[...] + _dot(a_ref[...], w_ref[...])


def _out_proj(x2, merged, w_out, tm, tn):
    m = x2.shape[0]
    return pl.pallas_call(
        _out_proj_kernel,
        grid=(m // tm, D_MODEL // tn),
        in_specs=[
            pl.BlockSpec((tm, tn), lambda i, j: (i, j)),
            pl.BlockSpec((tm, D_MODEL), lambda i, j: (i, 0)),
            pl.BlockSpec((D_MODEL, tn), lambda i, j: (0, j)),
        ],
        out_specs=pl.BlockSpec((tm, tn), lambda i, j: (i, j)),
        out_shape=jax.ShapeDtypeStruct((m, D_MODEL), F32),
        compiler_params=_params("parallel", "arbitrary"),
        name="out_proj",
    )(x2, merged, w_out)


MLA_COLS = Q_LORA + KV_LORA + QK_ROPE
SHIFTED_COL0 = 2 * BRANCH_W + MLA_COLS
LANE_SHIFT = SHIFTED_COL0 % LANES
SHIFTED_ORDER = (COL_MZ, COL_CB, COL_CC, COL_CX, COL_CZ, COL_XQ, COL_XZ) + tuple(
    COL_GATE + g for g in range(N_BRANCH * D_MODEL // BRANCH_W))


def _w_relayout_kernel(mb_ref, eb_ref, main_ref, extra_ref, o_ref):
    del mb_ref, eb_ref
    j = pl.program_id(2)
    aligned = (j == COL_PV) | (j == COL_PZ)

    @pl.when(aligned)
    def _():
        o_ref[...] = main_ref[...].astype(BF16)

    @pl.when(jnp.logical_not(aligned))
    def _():
        cat = jnp.concatenate([main_ref[...], extra_ref[...]], axis=1)
        o_ref[...] = cat[:, LANE_SHIFT:LANE_SHIFT + BRANCH_W].astype(BF16)


def _w_relayout(w_in, tr):
    depth = w_in.shape[0]
    n_tiles = N_MAIN // BRANCH_W
    base = (SHIFTED_COL0 - LANE_SHIFT) // BRANCH_W
    main_blk, extra_blk = [0] * n_tiles, [0] * n_tiles
    main_blk[COL_PV], main_blk[COL_PZ] = 0, 1
    for r, slot in enumerate(SHIFTED_ORDER):
        main_blk[slot] = base + r
        extra_blk[slot] = (base + r + 1) * (BRANCH_W // LANES)
    grid_spec = pltpu.PrefetchScalarGridSpec(
        num_scalar_prefetch=2,
        grid=(depth, D_MODEL // tr, n_tiles),
        in_specs=[
            pl.BlockSpec((None, tr, BRANCH_W), lambda l, r, j, mb, eb: (l, r, mb[j])),
            pl.BlockSpec((None, tr, LANES), lambda l, r, j, mb, eb: (l, r, eb[j])),
        ],
        out_specs=pl.BlockSpec((None, tr, BRANCH_W), lambda l, r, j, mb, eb: (l, r, j)),
    )
    return pl.pallas_call(
        _w_relayout_kernel,
        grid_spec=grid_spec,
        out_shape=jax.ShapeDtypeStruct((depth, D_MODEL, N_MAIN), BF16),
        compiler_params=_params("parallel", "parallel", "arbitrary"),
        name="w_in_relayout",
    )(jnp.asarray(main_blk, jnp.int32), jnp.asarray(extra_blk, jnp.int32), w_in, w_in)


def _w_lat_relayout_kernel(qkv_ref, kr_ref, o_ref):
    o_ref[:, :Q_LORA + KV_LORA] = qkv_ref[...].astype(BF16)
    lane = lax.broadcasted_iota(jnp.int32, kr_ref.shape, 1)
    o_ref[:, Q_LORA + KV_LORA:] = jnp.where(lane < QK_ROPE, kr_ref[...], 0.0).astype(BF16)


def _w_lat_relayout(w_in, tr):
    depth = w_in.shape[0]
    lat0 = 2 * BRANCH_W
    assert lat0 % (Q_LORA + KV_LORA) == 0 and (lat0 + Q_LORA + KV_LORA) % LANES == 0
    return pl.pallas_call(
        _w_lat_relayout_kernel,
        grid=(depth, D_MODEL // tr),
        in_specs=[
            pl.BlockSpec((None, tr, Q_LORA + KV_LORA), lambda l, r: (l, r, lat0 // (Q_LORA + KV_LORA))),
            pl.BlockSpec((None, tr, LANES), lambda l, r: (l, r, (lat0 + Q_LORA + KV_LORA) // LANES)),
        ],
        out_specs=pl.BlockSpec((None, tr, LAT_W), lambda l, r: (l, r, 0)),
        out_shape=jax.ShapeDtypeStruct((depth, D_MODEL, LAT_W), BF16),
        compiler_params=_params("parallel", "parallel"),
        name="w_lat_relayout",
    )(w_in, w_in)


def _layer_weights(l, norm_g, gate_b, pool_w, pool_scale, q_a_norm_g, kv_a_norm_g, w_uq, w_ukv,
                   mla_q_norm_g, mla_k_norm_g, conv_w, mem_norm_g, w_mem_kv, xattn_q_norm_g, xattn_k_norm_g,
                   w_branch, w_out):
    b_main = jnp.concatenate([jnp.zeros(((N_PLAIN + N_SILU) * BRANCH_W,), F32), gate_b[l]])[None, :]
    w_uq_p = jnp.pad(w_uq[l].reshape(Q_LORA, MLA_HEADS, QK_HEAD), ((0, 0), (0, 0), (0, HEAD_PAD - QK_HEAD)))
    w_uq_p = w_uq_p.reshape(Q_LORA, MLA_HEADS * HEAD_PAD).astype(BF16)
    w_ukv3 = w_ukv[l].reshape(KV_LORA, MLA_HEADS, QK_NOPE + V_HEAD)
    w_ukv_p = jnp.concatenate([w_ukv3[:, :, :QK_NOPE].reshape(KV_LORA, MLA_HEADS * QK_NOPE),
                               w_ukv3[:, :, QK_NOPE:].reshape(KV_LORA, MLA_HEADS * V_HEAD)], axis=1).astype(BF16)
    qg_pad = jnp.pad(mla_q_norm_g[l], (0, HEAD_PAD - QK_HEAD))[None, :]
    kg_nope = mla_k_norm_g[l][None, :QK_NOPE]
    kg_tail = jnp.pad(mla_k_norm_g[l][QK_NOPE:], (0, LANES - QK_ROPE))[None, :]
    return dict(
        norm_g=norm_g[l][None, :], b_main=b_main,
        qag=q_a_norm_g[l][None, :], kvag=kv_a_norm_g[l][None, :], w_uq_p=w_uq_p, w_ukv_p=w_ukv_p,
        qg_pad=qg_pad, kg_nope=kg_nope, kg_tail=kg_tail,
        pool_w=pool_w[l].astype(BF16), pool_scale=pool_scale[l][None, :], conv_w=conv_w[l],
        mem_g=mem_norm_g[l][None, :], w_mem=w_mem_kv[l].astype(BF16),
        xq_g=xattn_q_norm_g[l][None, :], xk_g=xattn_k_norm_g[l][None, :],
        w_branch=w_branch[l].astype(BF16), w_out=w_out[l].astype(BF16))


def _tile(n, want):
    t = min(n, want)
    assert n % t == 0, (n, want)
    return t


def kernel(x, mem, positions, norm_g, w_in, gate_b, pool_w, pool_scale, q_a_norm_g, kv_a_norm_g, w_uq, w_ukv, mla_q_norm_g, mla_k_norm_g, conv_w, mem_norm_g, w_mem_kv, xattn_q_norm_g, xattn_k_norm_g, w_branch, w_out):
    batch, seq, _ = x.shape
    m = batch * seq
    depth = w_in.shape[0]
    assert mem.shape[1] == MEM_LEN and seq % LANES == 0
    x2 = x.reshape(m, D_MODEL)
    mem2 = mem.reshape(batch * MEM_LEN, D_MODEL)
    t_big = _tile(seq, 1024)
    t_mid = _tile(seq, 512)
    tabs = _rope_tables(positions, t_big)
    w_main_all = _w_relayout(w_in, 1024)
    w_lat_all = _w_lat_relayout(w_in, 1024)
    for l in range(depth):
        p = _layer_weights(l, norm_g, gate_b, pool_w, pool_scale, q_a_norm_g, kv_a_norm_g, w_uq, w_ukv,
                           mla_q_norm_g, mla_k_norm_g, conv_w, mem_norm_g, w_mem_kv, xattn_q_norm_g,
                           xattn_k_norm_g, w_branch, w_out)
        proj = _in_proj(x2, p["norm_g"], w_main_all, l, p["b_main"], t_big, BRANCH_W)
        qt, k, vt = _mla_prep(x2, p["norm_g"], w_lat_all, l, p["qag"], p["kvag"], p["w_uq_p"], p["w_ukv_p"],
                              p["qg_pad"], p["kg_nope"], p["kg_tail"], tabs, t_mid)
        y_mla = _attention(qt, k, vt, proj, batch, seq, t_mid, ATTN_HEADS_PER_STEP)
        k_mem, v_mem = _mem_kv(mem2, p["mem_g"], p["w_mem"], p["xk_g"])
        y_pool, y_conv, y_mem = _mixers(proj, p["pool_w"], p["pool_scale"], p["conv_w"], p["xq_g"], k_mem, v_mem,
                                        seq, t_mid)
        merged = _merge((y_pool, y_mla, y_conv, y_mem), proj, p["w_branch"], t_big, 512)
        x2 = _out_proj(x2, merged, p["w_out"], t_big, 1024)
    return x2.reshape(batch, seq, D_MODEL)
```

```python
import functools

import jax
import jax.numpy as jnp
from jax import lax
from jax.experimental import pallas as pl
from jax.experimental.pallas import tpu as pltpu

D_MODEL = 2048
MEM_LEN = 256
N_BRANCH = 4
BRANCH_W = 1024
POOL_GROUPS = 4
POOL_WINDOWS = (2, 4, 8, 16)
POOL_GW = BRANCH_W // POOL_GROUPS
MLA_HEADS = 8
Q_LORA = 512
KV_LORA = 512
QK_NOPE = 128
QK_ROPE = 64
QK_HEAD = QK_NOPE + QK_ROPE
V_HEAD = 128
ROPE_THETA = 10000.0
CONV_W = 3
XATTN_HEADS = 4
XATTN_HEAD_DIM = BRANCH_W // XATTN_HEADS
EPS = 1e-6

LANES = 128
HEAD_PAD = 2 * LANES
HALO = 16
LAT_W = Q_LORA + KV_LORA + LANES
VMEM_LIMIT = 56 * 1024 * 1024
LOG2_E = 1.4426950408889634
ATTN_HEADS_PER_STEP = 2

(COL_PV, COL_CB, COL_CC, COL_CX, COL_XQ, COL_PZ, COL_MZ, COL_CZ, COL_XZ, COL_GATE) = range(10)
N_PLAIN, N_SILU = 5, 4
N_MAIN = (N_PLAIN + N_SILU) * BRANCH_W + N_BRANCH * D_MODEL

BF16 = jnp.bfloat16
F32 = jnp.float32


def _params(*sem):
    return pltpu.CompilerParams(dimension_semantics=sem, vmem_limit_bytes=VMEM_LIMIT)


def _sigmoid(v):
    return 0.5 * jnp.tanh(0.5 * v) + 0.5


def _rms(v, n):
    return lax.rsqrt(jnp.sum(v * v, axis=-1, keepdims=True) * (1.0 / n) + EPS)


def _dot(a, b):
    return jnp.dot(a, b, preferred_element_type=F32)


def _dot_nt(a, b):
    return lax.dot_general(a, b, (((1,), (1,)), ((), ())), preferred_element_type=F32)


def _rope_table_kernel(pos_ref, inv_ref, c_ref, s1_ref, s2_ref):
    ang = pos_ref[...].astype(F32) * inv_ref[...]
    lane = lax.broadcasted_iota(jnp.int32, ang.shape, 1)
    cos, sin = jnp.cos(ang), jnp.sin(ang)
    c_ref[...] = jnp.where(lane < QK_ROPE, cos, 0.0)
    s1_ref[...] = jnp.where((lane >= QK_ROPE // 2) & (lane < QK_ROPE), sin, 0.0)
    s2_ref[...] = jnp.where(lane < QK_ROPE // 2, -sin, 0.0)


def _rope_tables(positions, tm):
    m = positions.size
    inv = ROPE_THETA ** (-jnp.arange(0, QK_ROPE, 2, dtype=F32) / QK_ROPE)
    inv_tile = jnp.concatenate([inv, inv, jnp.zeros((LANES - QK_ROPE,), F32)])[None, :]
    tab = jax.ShapeDtypeStruct((m, LANES), F32)
    return pl.pallas_call(
        _rope_table_kernel,
        grid=(m // tm,),
        in_specs=[pl.BlockSpec((tm, 1), lambda i: (i, 0)), pl.BlockSpec((1, LANES), lambda i: (0, 0))],
        out_specs=[pl.BlockSpec((tm, LANES), lambda i: (i, 0))] * 3,
        out_shape=[tab, tab, tab],
        compiler_params=_params("parallel"),
        name="rope_tables",
    )(positions.reshape(m, 1), inv_tile)


def _in_proj_kernel(x_ref, g_ref, w_ref, b_ref, o_ref, h_ref):
    j = pl.program_id(1)

    @pl.when(j == 0)
    def _():
        x = x_ref[...]
        h_ref[...] = (x * _rms(x, D_MODEL) * g_ref[...]).astype(BF16)

    @pl.when(j < N_PLAIN)
    def _():
        o_ref[...] = _dot(h_ref[...], w_ref[...]).astype(BF16)

    @pl.when((j >= N_PLAIN) & (j < N_PLAIN + N_SILU))
    def _():
        acc = _dot(h_ref[...], w_ref[...])
        o_ref[...] = (acc * _sigmoid(acc)).astype(BF16)

    @pl.when(j >= N_PLAIN + N_SILU)
    def _():
        acc = _dot(h_ref[...], w_ref[...])
        o_ref[...] = _sigmoid(acc + b_ref[...]).astype(BF16)


def _in_proj(x2, norm_g, w_main, layer, b_main, tm, tn):
    m = x2.shape[0]
    return pl.pallas_call(
        _in_proj_kernel,
        grid=(m // tm, N_MAIN // tn),
        in_specs=[
            pl.BlockSpec((tm, D_MODEL), lambda i, j: (i, 0)),
            pl.BlockSpec((1, D_MODEL), lambda i, j: (0, 0)),
            pl.BlockSpec((None, D_MODEL, tn), lambda i, j: (layer, 0, j)),
            pl.BlockSpec((1, tn), lambda i, j: (0, j)),
        ],
        out_specs=pl.BlockSpec((tm, tn), lambda i, j: (i, j)),
        out_shape=jax.ShapeDtypeStruct((m, N_MAIN), BF16),
        scratch_shapes=[pltpu.VMEM((tm, D_MODEL), BF16)],
        compiler_params=_params("parallel", "arbitrary"),
        name="in_proj",
    )(x2, norm_g, w_main, b_main)


def _rope_tile(t, c, s1, s2):
    return t * c + pltpu.roll(t, QK_ROPE // 2, 1) * s1 + pltpu.roll(t, LANES - QK_ROPE // 2, 1) * s2


def _mla_prep_kernel(x_ref, g_ref, wlat_ref, qag_ref, kvag_ref, wuq_ref, wukv_ref, qg_ref, kgn_ref, kgt_ref,
                     c_ref, s1_ref, s2_ref, qt_out, k_out, vt_out, *, n_split):
    rows_per = x_ref.shape[0] // n_split
    for part in range(n_split):
        rows = slice(part * rows_per, (part + 1) * rows_per)
        x = x_ref[rows, :]
        h = (x * _rms(x, D_MODEL) * g_ref[...]).astype(BF16)
        lat = _dot(h, wlat_ref[...])
        cq = lat[:, :Q_LORA]
        ckv = lat[:, Q_LORA:Q_LORA + KV_LORA]
        kr = lat[:, Q_LORA + KV_LORA:]
        q = _dot((cq * _rms(cq, Q_LORA) * qag_ref[...]).astype(BF16), wuq_ref[...])
        kv = _dot((ckv * _rms(ckv, KV_LORA) * kvag_ref[...]).astype(BF16), wukv_ref[...])
        c, s1, s2 = c_ref[rows, :], s1_ref[rows, :], s2_ref[rows, :]
        kr_ss = jnp.sum(kr * kr, axis=-1, keepdims=True)
        kr_rot = _rope_tile(kr * kgt_ref[...], c, s1, s2)
        qg = qg_ref[...]
        scale = QK_HEAD ** -0.5 * LOG2_E
        for hd in range(MLA_HEADS):
            qh = q[:, hd * HEAD_PAD:(hd + 1) * HEAD_PAD]
            qn = qh * (_rms(qh, QK_HEAD) * scale) * qg
            qt_out[hd * HEAD_PAD:hd * HEAD_PAD + LANES, rows] = qn[:, :LANES].T.astype(BF16)
            qt_out[hd * HEAD_PAD + LANES:(hd + 1) * HEAD_PAD, rows] = (
                _rope_tile(qn[:, LANES:], c, s1, s2).T.astype(BF16))
            kn = kv[:, hd * QK_NOPE:(hd + 1) * QK_NOPE]
            rk = lax.rsqrt((jnp.sum(kn * kn, axis=-1, keepdims=True) + kr_ss) * (1.0 / QK_HEAD) + EPS)
            k_out[rows, hd * HEAD_PAD:hd * HEAD_PAD + LANES] = (kn * rk * kgn_ref[...]).astype(BF16)
            k_out[rows, hd * HEAD_PAD + LANES:(hd + 1) * HEAD_PAD] = (kr_rot * rk).astype(BF16)
            vh = kv[:, MLA_HEADS * QK_NOPE + hd * V_HEAD:MLA_HEADS * QK_NOPE + (hd + 1) * V_HEAD]
            vt_out[hd * V_HEAD:(hd + 1) * V_HEAD, rows] = vh.T.astype(BF16)


def _mla_prep(x2, norm_g, w_lat, layer, qag, kvag, w_uq_p, w_ukv_p, qg_pad, kg_nope, kg_tail, tabs, tm):
    m = x2.shape[0]
    row = lambda w: pl.BlockSpec((tm, w), lambda i: (i, 0))
    full = lambda a: pl.BlockSpec(a.shape, lambda i: (0,) * a.ndim)
    col = lambda h: pl.BlockSpec((h, tm), lambda i: (0, i))
    lat = pl.BlockSpec((None, D_MODEL, LAT_W), lambda i: (layer, 0, 0))
    consts = (norm_g, w_lat, qag, kvag, w_uq_p, w_ukv_p, qg_pad, kg_nope, kg_tail)
    return pl.pallas_call(
        functools.partial(_mla_prep_kernel, n_split=2),
        grid=(m // tm,),
        in_specs=[row(D_MODEL)] + [lat if a is w_lat else full(a) for a in consts] + [row(LANES)] * 3,
        out_specs=[col(MLA_HEADS * HEAD_PAD), row(MLA_HEADS * HEAD_PAD), col(MLA_HEADS * V_HEAD)],
        out_shape=[jax.ShapeDtypeStruct((MLA_HEADS * HEAD_PAD, m), BF16),
                   jax.ShapeDtypeStruct((m, MLA_HEADS * HEAD_PAD), BF16),
                   jax.ShapeDtypeStruct((MLA_HEADS * V_HEAD, m), BF16)],
        compiler_params=_params("parallel"),
        name="mla_prep",
    )(x2, *consts, *tabs)


def _attn_kernel(qt_ref, k_ref, vt_ref, z_ref, o_ref, acc_ref, sa_ref, sb_ref, *, t, heads):
    qi = pl.program_id(2)
    acc_ref[...] = jnp.zeros(acc_ref.shape, F32)

    def scores(kj, s_ref):
        start = pl.multiple_of(kj * t, t)
        for g in range(heads):
            s_ref[g] = _dot(k_ref[pl.ds(start, t), g * HEAD_PAD:(g + 1) * HEAD_PAD],
                            qt_ref[g * HEAD_PAD:(g + 1) * HEAD_PAD, :])

    def consume(kj, s_ref, stats, masked):
        start = pl.multiple_of(kj * t, t)
        out = []
        for g in range(heads):
            m_prev, l_prev = stats[2 * g], stats[2 * g + 1]
            st = s_ref[g]
            if masked:
                kpos = kj * t + lax.broadcasted_iota(jnp.int32, (t, t), 0)
                qpos = qi * t + lax.broadcasted_iota(jnp.int32, (t, t), 1)
                st = jnp.where(kpos <= qpos, st, -jnp.inf)
            m_new = jnp.maximum(m_prev, jnp.max(st, axis=0, keepdims=True))
            alpha = jnp.exp2(m_prev - m_new)
            pt = jnp.exp2(st - m_new)
            l_new = alpha * l_prev + jnp.sum(pt, axis=0, keepdims=True)
            pv = _dot(vt_ref[g * V_HEAD:(g + 1) * V_HEAD, pl.ds(start, t)], pt.astype(BF16))
            acc_ref[g] = alpha * acc_ref[g] + pv
            out += [m_new, l_new]
        return tuple(out)

    def pair(i, stats):
        scores(2 * i + 1, sb_ref)
        stats = consume(2 * i, sa_ref, stats, False)
        scores(2 * i + 2, sa_ref)
        return consume(2 * i + 1, sb_ref, stats, False)

    init = (jnp.full((1, t), -jnp.inf, F32), jnp.zeros((1, t), F32)) * heads
    scores(0, sa_ref)
    stats = lax.fori_loop(0, qi // 2, pair, init)

    def tail_odd(stats):
        scores(qi, sb_ref)
        return consume(qi, sb_ref, consume(qi - 1, sa_ref, stats, False), True)

    stats = lax.cond(qi % 2 == 1, tail_odd, lambda stats: consume(qi, sa_ref, stats, True), stats)
    for g in range(heads):
        o = (acc_ref[g] / stats[2 * g + 1]).T
        o_ref[:, g * V_HEAD:(g + 1) * V_HEAD] = (o * z_ref[:, g * V_HEAD:(g + 1) * V_HEAD].astype(F32)).astype(BF16)


def _attention(qt, k, vt, proj, batch, seq, t, heads):
    m = k.shape[0]
    tq = t
    nq = seq // tq
    z_col = COL_MZ * BRANCH_W // (heads * V_HEAD)
    return pl.pallas_call(
        functools.partial(_attn_kernel, t=t, heads=heads),
        grid=(batch, MLA_HEADS // heads, nq),
        in_specs=[
            pl.BlockSpec((heads * HEAD_PAD, tq), lambda b, h, i: (h, b * nq + i)),
            pl.BlockSpec((seq, heads * HEAD_PAD), lambda b, h, i: (b, h)),
            pl.BlockSpec((heads * V_HEAD, seq), lambda b, h, i: (h, b)),
            pl.BlockSpec((tq, heads * V_HEAD), lambda b, h, i: (b * nq + i, z_col + h)),
        ],
        out_specs=pl.BlockSpec((tq, heads * V_HEAD), lambda b, h, i: (b * nq + i, h)),
        out_shape=jax.ShapeDtypeStruct((m, MLA_HEADS * V_HEAD), BF16),
        scratch_shapes=[pltpu.VMEM((heads, V_HEAD, t), F32),
                        pltpu.VMEM((heads, t, t), F32), pltpu.VMEM((heads, t, t), F32)],
        compiler_params=_params("parallel", "parallel", "arbitrary"),
        name="mla_attention",
    )(qt, k, vt, proj)


def _mem_kv_kernel(mem_ref, g_ref, w_ref, kg_ref, k_out, v_out):
    mm = mem_ref[...]
    kvm = _dot((mm * _rms(mm, D_MODEL) * g_ref[...]).astype(BF16), w_ref[...])
    for hd in range(XATTN_HEADS):
        kh = kvm[:, hd * XATTN_HEAD_DIM:(hd + 1) * XATTN_HEAD_DIM]
        k_out[:, hd * XATTN_HEAD_DIM:(hd + 1) * XATTN_HEAD_DIM] = (
            kh * _rms(kh, XATTN_HEAD_DIM) * kg_ref[...]).astype(BF16)
    v_out[...] = kvm[:, BRANCH_W:].astype(BF16)


def _mem_kv(mem2, mem_g, w_mem, kg):
    m = mem2.shape[0]
    out = jax.ShapeDtypeStruct((m, BRANCH_W), BF16)
    return pl.pallas_call(
        _mem_kv_kernel,
        grid=(m // MEM_LEN,),
        in_specs=[
            pl.BlockSpec((MEM_LEN, D_MODEL), lambda b: (b, 0)),
            pl.BlockSpec((1, D_MODEL), lambda b: (0, 0)),
            pl.BlockSpec((D_MODEL, 2 * BRANCH_W), lambda b: (0, 0)),
            pl.BlockSpec((1, XATTN_HEAD_DIM), lambda b: (0, 0)),
        ],
        out_specs=[pl.BlockSpec((MEM_LEN, BRANCH_W), lambda b: (b, 0))] * 2,
        out_shape=[out, out],
        compiler_params=_params("parallel"),
        name="mem_kv",
    )(mem2, mem_g, w_mem, kg)


def _mixers_kernel(pv_ref, cb_ref, cc_ref, cx_ref, xq_ref, pz_ref, cz_ref, xz_ref,
                   pvh_ref, cch_ref, cxh_ref, pw_ref, ps_ref, cw_ref, xqg_ref, km_ref, vm_ref,
                   ypool_ref, yconv_ref, ymem_ref, *, tm, tiles_per_seq):
    t0 = (pl.program_id(0) % tiles_per_seq) * tm
    has_prev = t0 > 0
    row = lax.broadcasted_iota(jnp.int32, (tm, 1), 0)

    r_i = lax.broadcasted_iota(jnp.int32, (tm, tm), 0)
    c_i = lax.broadcasted_iota(jnp.int32, (tm, tm), 1)
    rh_i = lax.broadcasted_iota(jnp.int32, (tm, HALO), 0)
    ch_i = lax.broadcasted_iota(jnp.int32, (tm, HALO), 1)
    pos = (t0 + row).astype(F32)
    for g, win in enumerate(POOL_WINDOWS):
        sl = slice(g * POOL_GW, (g + 1) * POOL_GW)
        vg = pv_ref[:, sl]
        band = ((c_i <= r_i) & (r_i - c_i < win)).astype(BF16)
        band_h = ((rh_i + (HALO - ch_i) < win) & has_prev).astype(BF16)
        wsum = _dot(band, vg) + _dot(band_h, pvh_ref[:, sl])
        cnt = jnp.minimum(pos + 1.0, float(win))
        mixed = wsum / cnt - vg.astype(F32)
        out = _dot(mixed.astype(BF16), pw_ref[g])
        ypool_ref[:, sl] = (out * ps_ref[:, sl] * pz_ref[:, sl].astype(F32)).astype(BF16)

    u = cc_ref[...].astype(F32) * cx_ref[...].astype(F32)
    uh = cch_ref[...].astype(F32) * cxh_ref[...].astype(F32)
    uh = jnp.where(has_prev, uh, 0.0)
    u1 = jnp.where(row == 0, uh[HALO - 1:HALO, :], pltpu.roll(u, 1, 0))
    u2 = jnp.where(row == 0, uh[HALO - 2:HALO - 1, :],
                   jnp.where(row == 1, uh[HALO - 1:HALO, :], pltpu.roll(u, 2, 0)))
    y = cw_ref[0:1, :] * u2 + cw_ref[1:2, :] * u1 + cw_ref[2:3, :] * u
    yconv_ref[...] = (cb_ref[...].astype(F32) * y * cz_ref[...].astype(F32)).astype(BF16)

    scale = XATTN_HEAD_DIM ** -0.5
    for hd in range(XATTN_HEADS):
        sl = slice(hd * XATTN_HEAD_DIM, (hd + 1) * XATTN_HEAD_DIM)
        qh = xq_ref[:, sl].astype(F32)
        qn = (qh * (_rms(qh, XATTN_HEAD_DIM) * scale) * xqg_ref[...]).astype(BF16)
        s = _dot_nt(qn, km_ref[:, sl])
        p = jnp.exp(s - jnp.max(s, axis=-1, keepdims=True))
        o = _dot(p.astype(BF16), vm_ref[:, sl]) / jnp.sum(p, axis=-1, keepdims=True)
        ymem_ref[:, sl] = (o * xz_ref[:, sl].astype(F32)).astype(BF16)


def _mixers(proj, pool_w, pool_scale, conv_w, xq_g, k_mem, v_mem, seq, tm):
    m = proj.shape[0]
    tiles_per_seq = seq // tm
    col = lambda c: pl.BlockSpec((tm, BRANCH_W), lambda i: (i, c))
    halo = lambda c: pl.BlockSpec((HALO, BRANCH_W), lambda i: (jnp.maximum(i * (tm // HALO) - 1, 0), c))
    full = lambda a: pl.BlockSpec(a.shape, lambda i: (0,) * a.ndim)
    memb = pl.BlockSpec((MEM_LEN, BRANCH_W), lambda i: (i // tiles_per_seq, 0))
    out = jax.ShapeDtypeStruct((m, BRANCH_W), BF16)
    return pl.pallas_call(
        functools.partial(_mixers_kernel, tm=tm, tiles_per_seq=tiles_per_seq),
        grid=(m // tm,),
        in_specs=[col(COL_PV), col(COL_CB), col(COL_CC), col(COL_CX), col(COL_XQ), col(COL_PZ), col(COL_CZ),
                  col(COL_XZ), halo(COL_PV), halo(COL_CC), halo(COL_CX),
                  full(pool_w), full(pool_scale), full(conv_w), full(xq_g), memb, memb],
        out_specs=[pl.BlockSpec((tm, BRANCH_W), lambda i: (i, 0))] * 3,
        out_shape=[out, out, out],
        compiler_params=_params("parallel"),
        name="mixers",
    )(*([proj] * 11), pool_w, pool_scale, conv_w, xq_g, k_mem, v_mem)


def _merge_kernel(y0_ref, y1_ref, y2_ref, y3_ref, g0_ref, g1_ref, g2_ref, g3_ref, wb_ref, o_ref):
    acc = None
    for b, (y_ref, g_ref) in enumerate(((y0_ref, g0_ref), (y1_ref, g1_ref), (y2_ref, g2_ref), (y3_ref, g3_ref))):
        term = g_ref[...].astype(F32) * _dot(y_ref[...], wb_ref[b])
        acc = term if acc is None else acc + term
    o_ref[...] = acc.astype(BF16)


def _merge(ys, proj, w_branch, tm, tn):
    m = proj.shape[0]
    gate0 = COL_GATE * BRANCH_W // tn
    gate = lambda b: pl.BlockSpec((tm, tn), lambda i, j: (i, gate0 + b * (D_MODEL // tn) + j))
    return pl.pallas_call(
        _merge_kernel,
        grid=(m // tm, D_MODEL // tn),
        in_specs=[pl.BlockSpec((tm, BRANCH_W), lambda i, j: (i, 0))] * N_BRANCH
        + [gate(b) for b in range(N_BRANCH)]
        + [pl.BlockSpec((N_BRANCH, BRANCH_W, tn), lambda i, j: (0, 0, j))],
        out_specs=pl.BlockSpec((tm, tn), lambda i, j: (i, j)),
        out_shape=jax.ShapeDtypeStruct((m, D_MODEL), BF16),
        compiler_params=_params("parallel", "arbitrary"),
        name="gated_merge",
    )(*ys, *([proj] * N_BRANCH), w_branch)


def _out_proj_kernel(x_ref, a_ref, w_ref, o_ref):
    o_ref[...] = x_ref[...] + _dot(a_ref[...], w_ref[...])


def _out_proj(x2, merged, w_out, tm, tn):
    m = x2.shape[0]
    return pl.pallas_call(
        _out_proj_kernel,
        grid=(m // tm, D_MODEL // tn),
        in_specs=[
            pl.BlockSpec((tm, tn), lambda i, j: (i, j)),
            pl.BlockSpec((tm, D_MODEL), lambda i, j: (i, 0)),
            pl.BlockSpec((D_MODEL, tn), lambda i, j: (0, j)),
        ],
        out_specs=pl.BlockSpec((tm, tn), lambda i, j: (i, j)),
        out_shape=jax.ShapeDtypeStruct((m, D_MODEL), F32),
        compiler_params=_params("parallel", "arbitrary"),
        name="out_proj",
    )(x2, merged, w_out)


MLA_COLS = Q_LORA + KV_LORA + QK_ROPE
SHIFTED_COL0 = 2 * BRANCH_W + MLA_COLS
LANE_SHIFT = SHIFTED_COL0 % LANES
SHIFTED_ORDER = (COL_MZ, COL_CB, COL_CC, COL_CX, COL_CZ, COL_XQ, COL_XZ) + tuple(
    COL_GATE + g for g in range(N_BRANCH * D_MODEL // BRANCH_W))


def _w_relayout_kernel(mb_ref, eb_ref, main_ref, extra_ref, o_ref):
    del mb_ref, eb_ref
    j = pl.program_id(2)
    aligned = (j == COL_PV) | (j == COL_PZ)

    @pl.when(aligned)
    def _():
        o_ref[...] = main_ref[...]

    @pl.when(jnp.logical_not(aligned))
    def _():
        cat = jnp.concatenate([main_ref[...].astype(F32), extra_ref[...].astype(F32)], axis=1)
        o_ref[...] = cat[:, LANE_SHIFT:LANE_SHIFT + BRANCH_W].astype(BF16)


def _w_relayout(w_in, tr):
    depth = w_in.shape[0]
    n_tiles = N_MAIN // BRANCH_W
    base = (SHIFTED_COL0 - LANE_SHIFT) // BRANCH_W
    main_blk, extra_blk = [0] * n_tiles, [0] * n_tiles
    main_blk[COL_PV], main_blk[COL_PZ] = 0, 1
    for r, slot in enumerate(SHIFTED_ORDER):
        main_blk[slot] = base + r
        extra_blk[slot] = (base + r + 1) * (BRANCH_W // LANES)
    grid_spec = pltpu.PrefetchScalarGridSpec(
        num_scalar_prefetch=2,
        grid=(depth, D_MODEL // tr, n_tiles),
        in_specs=[
            pl.BlockSpec((None, tr, BRANCH_W), lambda l, r, j, mb, eb: (l, r, mb[j])),
            pl.BlockSpec((None, tr, LANES), lambda l, r, j, mb, eb: (l, r, eb[j])),
        ],
        out_specs=pl.BlockSpec((None, tr, BRANCH_W), lambda l, r, j, mb, eb: (l, r, j)),
    )
    return pl.pallas_call(
        _w_relayout_kernel,
        grid_spec=grid_spec,
        out_shape=jax.ShapeDtypeStruct((depth, D_MODEL, N_MAIN), BF16),
        compiler_params=_params("parallel", "parallel", "arbitrary"),
        name="w_in_relayout",
    )(jnp.asarray(main_blk, jnp.int32), jnp.asarray(extra_blk, jnp.int32), w_in, w_in)


def _w_lat_relayout_kernel(qkv_ref, kr_ref, o_ref):
    o_ref[:, :Q_LORA + KV_LORA] = qkv_ref[...]
    lane = lax.broadcasted_iota(jnp.int32, kr_ref.shape, 1)
    o_ref[:, Q_LORA + KV_LORA:] = jnp.where(lane < QK_ROPE, kr_ref[...].astype(F32), 0.0).astype(BF16)


def _w_lat_relayout(w_in, tr):
    depth = w_in.shape[0]
    lat0 = 2 * BRANCH_W
    assert lat0 % (Q_LORA + KV_LORA) == 0 and (lat0 + Q_LORA + KV_LORA) % LANES == 0
    return pl.pallas_call(
        _w_lat_relayout_kernel,
        grid=(depth, D_MODEL // tr),
        in_specs=[
            pl.BlockSpec((None, tr, Q_LORA + KV_LORA), lambda l, r: (l, r, lat0 // (Q_LORA + KV_LORA))),
            pl.BlockSpec((None, tr, LANES), lambda l, r: (l, r, (lat0 + Q_LORA + KV_LORA) // LANES)),
        ],
        out_specs=pl.BlockSpec((None, tr, LAT_W), lambda l, r: (l, r, 0)),
        out_shape=jax.ShapeDtypeStruct((depth, D_MODEL, LAT_W), BF16),
        compiler_params=_params("parallel", "parallel"),
        name="w_lat_relayout",
    )(w_in, w_in)


def _layer_weights(l, norm_g, gate_b, pool_w, pool_scale, q_a_norm_g, kv_a_norm_g, w_uq, w_ukv,
                   mla_q_norm_g, mla_k_norm_g, conv_w, mem_norm_g, w_mem_kv, xattn_q_norm_g, xattn_k_norm_g,
                   w_branch, w_out):
    b_main = jnp.concatenate([jnp.zeros(((N_PLAIN + N_SILU) * BRANCH_W,), F32), gate_b[l]])[None, :]
    w_uq_p = jnp.pad(w_uq[l].reshape(Q_LORA, MLA_HEADS, QK_HEAD), ((0, 0), (0, 0), (0, HEAD_PAD - QK_HEAD)))
    w_uq_p = w_uq_p.reshape(Q_LORA, MLA_HEADS * HEAD_PAD).astype(BF16)
    w_ukv3 = w_ukv[l].reshape(KV_LORA, MLA_HEADS, QK_NOPE + V_HEAD)
    w_ukv_p = jnp.concatenate([w_ukv3[:, :, :QK_NOPE].reshape(KV_LORA, MLA_HEADS * QK_NOPE),
                               w_ukv3[:, :, QK_NOPE:].reshape(KV_LORA, MLA_HEADS * V_HEAD)], axis=1).astype(BF16)
    qg_pad = jnp.pad(mla_q_norm_g[l], (0, HEAD_PAD - QK_HEAD))[None, :]
    kg_nope = mla_k_norm_g[l][None, :QK_NOPE]
    kg_tail = jnp.pad(mla_k_norm_g[l][QK_NOPE:], (0, LANES - QK_ROPE))[None, :]
    return dict(
        norm_g=norm_g[l][None, :], b_main=b_main,
        qag=q_a_norm_g[l][None, :], kvag=kv_a_norm_g[l][None, :], w_uq_p=w_uq_p, w_ukv_p=w_ukv_p,
        qg_pad=qg_pad, kg_nope=kg_nope, kg_tail=kg_tail,
        pool_w=pool_w[l].astype(BF16), pool_scale=pool_scale[l][None, :], conv_w=conv_w[l],
        mem_g=mem_norm_g[l][None, :], w_mem=w_mem_kv[l].astype(BF16),
        xq_g=xattn_q_norm_g[l][None, :], xk_g=xattn_k_norm_g[l][None, :],
        w_branch=w_branch[l].astype(BF16), w_out=w_out[l].astype(BF16))


def _tile(n, want):
    t = min(n, want)
    assert n % t == 0, (n, want)
    return t


def kernel(x, mem, positions, norm_g, w_in, gate_b, pool_w, pool_scale, q_a_norm_g, kv_a_norm_g, w_uq, w_ukv, mla_q_norm_g, mla_k_norm_g, conv_w, mem_norm_g, w_mem_kv, xattn_q_norm_g, xattn_k_norm_g, w_branch, w_out):
    batch, seq, _ = x.shape
    m = batch * seq
    depth = w_in.shape[0]
    assert mem.shape[1] == MEM_LEN and seq % LANES == 0
    x2 = x.reshape(m, D_MODEL)
    mem2 = mem.reshape(batch * MEM_LEN, D_MODEL)
    t_big = _tile(seq, 1024)
    t_mid = _tile(seq, 512)
    tabs = _rope_tables(positions, t_big)
    w_in_bf16 = w_in.astype(BF16)
    w_main_all = _w_relayout(w_in_bf16, 1024)
    w_lat_all = _w_lat_relayout(w_in_bf16, 1024)
    for l in range(depth):
        p = _layer_weights(l, norm_g, gate_b, pool_w, pool_scale, q_a_norm_g, kv_a_norm_g, w_uq, w_ukv,
                           mla_q_norm_g, mla_k_norm_g, conv_w, mem_norm_g, w_mem_kv, xattn_q_norm_g,
                           xattn_k_norm_g, w_branch, w_out)
        proj = _in_proj(x2, p["norm_g"], w_main_all, l, p["b_main"], t_big, BRANCH_W)
        qt, k, vt = _mla_prep(x2, p["norm_g"], w_lat_all, l, p["qag"], p["kvag"], p["w_uq_p"], p["w_ukv_p"],
                              p["qg_pad"], p["kg_nope"], p["kg_tail"], tabs, t_mid)
        y_mla = _attention(qt, k, vt, proj, batch, seq, t_mid, ATTN_HEADS_PER_STEP)
        k_mem, v_mem = _mem_kv(mem2, p["mem_g"], p["w_mem"], p["xk_g"])
        y_pool, y_conv, y_mem = _mixers(proj, p["pool_w"], p["pool_scale"], p["conv_w"], p["xq_g"], k_mem, v_mem,
                                        seq, t_mid)
        merged = _merge((y_pool, y_mla, y_conv, y_mem), proj, p["w_branch"], t_big, 512)
        x2 = _out_proj(x2, merged, p["w_out"], t_big, 1024)
    return x2.reshape(batch, seq, D_MODEL)
```

```python
import functools

import jax
import jax.numpy as jnp
from jax import lax
from jax.experimental import pallas as pl
from jax.experimental.pallas import tpu as pltpu

D_MODEL = 2048
MEM_LEN = 256
N_BRANCH = 4
BRANCH_W = 1024
POOL_GROUPS = 4
POOL_WINDOWS = (2, 4, 8, 16)
POOL_GW = BRANCH_W // POOL_GROUPS
MLA_HEADS = 8
Q_LORA = 512
KV_LORA = 512
QK_NOPE = 128
QK_ROPE = 64
QK_HEAD = QK_NOPE + QK_ROPE
V_HEAD = 128
ROPE_THETA = 10000.0
CONV_W = 3
XATTN_HEADS = 4
XATTN_HEAD_DIM = BRANCH_W // XATTN_HEADS
EPS = 1e-6

LANES = 128
HEAD_PAD = 2 * LANES
HALO = 16
LAT_W = Q_LORA + KV_LORA + LANES
VMEM_LIMIT = 56 * 1024 * 1024
LOG2_E = 1.4426950408889634
ATTN_HEADS_PER_STEP = 2
ONES_ROWS = 16

(COL_PV, COL_CB, COL_CC, COL_CX, COL_XQ, COL_PZ, COL_MZ, COL_CZ, COL_XZ, COL_GATE) = range(10)
N_PLAIN, N_SILU = 5, 4
N_MAIN = (N_PLAIN + N_SILU) * BRANCH_W + N_BRANCH * D_MODEL

BF16 = jnp.bfloat16
F32 = jnp.float32


def _params(*sem):
    return pltpu.CompilerParams(dimension_semantics=sem, vmem_limit_bytes=VMEM_LIMIT)


def _sigmoid(v):
    return 0.5 * jnp.tanh(0.5 * v) + 0.5


def _rms(v, n):
    return lax.rsqrt(jnp.sum(v * v, axis=-1, keepdims=True) * (1.0 / n) + EPS)


def _dot(a, b):
    return jnp.dot(a, b, preferred_element_type=F32)


def _dot_nt(a, b):
    return lax.dot_general(a, b, (((1,), (1,)), ((), ())), preferred_element_type=F32)


def _rope_table_kernel(pos_ref, inv_ref, c_ref, s1_ref, s2_ref):
    ang = pos_ref[...].astype(F32) * inv_ref[...]
    lane = lax.broadcasted_iota(jnp.int32, ang.shape, 1)
    cos, sin = jnp.cos(ang), jnp.sin(ang)
    c_ref[...] = jnp.where(lane < QK_ROPE, cos, 0.0)
    s1_ref[...] = jnp.where((lane >= QK_ROPE // 2) & (lane < QK_ROPE), sin, 0.0)
    s2_ref[...] = jnp.where(lane < QK_ROPE // 2, -sin, 0.0)


def _rope_tables(positions, tm):
    m = positions.size
    inv = ROPE_THETA ** (-jnp.arange(0, QK_ROPE, 2, dtype=F32) / QK_ROPE)
    inv_tile = jnp.concatenate([inv, inv, jnp.zeros((LANES - QK_ROPE,), F32)])[None, :]
    tab = jax.ShapeDtypeStruct((m, LANES), F32)
    return pl.pallas_call(
        _rope_table_kernel,
        grid=(m // tm,),
        in_specs=[pl.BlockSpec((tm, 1), lambda i: (i, 0)), pl.BlockSpec((1, LANES), lambda i: (0, 0))],
        out_specs=[pl.BlockSpec((tm, LANES), lambda i: (i, 0))] * 3,
        out_shape=[tab, tab, tab],
        compiler_params=_params("parallel"),
        name="rope_tables",
    )(positions.reshape(m, 1), inv_tile)


def _in_proj_kernel(x_ref, g_ref, w_ref, b_ref, o_ref, h_ref):
    j = pl.program_id(1)

    @pl.when(j == 0)
    def _():
        x = x_ref[...]
        h_ref[...] = (x * _rms(x, D_MODEL) * g_ref[...]).astype(BF16)

    @pl.when(j < N_PLAIN)
    def _():
        o_ref[...] = _dot(h_ref[...], w_ref[...]).astype(BF16)

    @pl.when((j >= N_PLAIN) & (j < N_PLAIN + N_SILU))
    def _():
        acc = _dot(h_ref[...], w_ref[...])
        o_ref[...] = (acc * _sigmoid(acc)).astype(BF16)

    @pl.when(j >= N_PLAIN + N_SILU)
    def _():
        acc = _dot(h_ref[...], w_ref[...])
        o_ref[...] = _sigmoid(acc + b_ref[...]).astype(BF16)


def _in_proj(x2, norm_g, w_main, layer, b_main, tm, tn):
    m = x2.shape[0]
    return pl.pallas_call(
        _in_proj_kernel,
        grid=(m // tm, N_MAIN // tn),
        in_specs=[
            pl.BlockSpec((tm, D_MODEL), lambda i, j: (i, 0)),
            pl.BlockSpec((1, D_MODEL), lambda i, j: (0, 0)),
            pl.BlockSpec((None, D_MODEL, tn), lambda i, j: (layer, 0, j)),
            pl.BlockSpec((1, tn), lambda i, j: (0, j)),
        ],
        out_specs=pl.BlockSpec((tm, tn), lambda i, j: (i, j)),
        out_shape=jax.ShapeDtypeStruct((m, N_MAIN), BF16),
        scratch_shapes=[pltpu.VMEM((tm, D_MODEL), BF16)],
        compiler_params=_params("parallel", "arbitrary"),
        name="in_proj",
    )(x2, norm_g, w_main, b_main)


def _rope_tile(t, c, s1, s2):
    return t * c + pltpu.roll(t, QK_ROPE // 2, 1) * s1 + pltpu.roll(t, LANES - QK_ROPE // 2, 1) * s2


def _mla_prep_kernel(x_ref, g_ref, wlat_ref, qag_ref, kvag_ref, wuq_ref, wukv_ref, qg_ref, kgn_ref, kgt_ref,
                     c_ref, s1_ref, s2_ref, qt_out, k_out, vt_out, *, n_split):
    rows_per = x_ref.shape[0] // n_split
    for part in range(n_split):
        rows = slice(part * rows_per, (part + 1) * rows_per)
        x = x_ref[rows, :]
        h = (x * _rms(x, D_MODEL) * g_ref[...]).astype(BF16)
        lat = _dot(h, wlat_ref[...])
        cq = lat[:, :Q_LORA]
        ckv = lat[:, Q_LORA:Q_LORA + KV_LORA]
        kr = lat[:, Q_LORA + KV_LORA:]
        q = _dot((cq * _rms(cq, Q_LORA) * qag_ref[...]).astype(BF16), wuq_ref[...])
        kv = _dot((ckv * _rms(ckv, KV_LORA) * kvag_ref[...]).astype(BF16), wukv_ref[...])
        c, s1, s2 = c_ref[rows, :], s1_ref[rows, :], s2_ref[rows, :]
        kr_ss = jnp.sum(kr * kr, axis=-1, keepdims=True)
        kr_rot = _rope_tile(kr * kgt_ref[...], c, s1, s2)
        qg = qg_ref[...]
        scale = QK_HEAD ** -0.5 * LOG2_E
        for hd in range(MLA_HEADS):
            qh = q[:, hd * HEAD_PAD:(hd + 1) * HEAD_PAD]
            qn = qh * (_rms(qh, QK_HEAD) * scale) * qg
            qt_out[hd * HEAD_PAD:hd * HEAD_PAD + LANES, rows] = qn[:, :LANES].T.astype(BF16)
            qt_out[hd * HEAD_PAD + LANES:(hd + 1) * HEAD_PAD, rows] = (
                _rope_tile(qn[:, LANES:], c, s1, s2).T.astype(BF16))
            kn = kv[:, hd * QK_NOPE:(hd + 1) * QK_NOPE]
            rk = lax.rsqrt((jnp.sum(kn * kn, axis=-1, keepdims=True) + kr_ss) * (1.0 / QK_HEAD) + EPS)
            k_out[rows, hd * HEAD_PAD:hd * HEAD_PAD + LANES] = (kn * rk * kgn_ref[...]).astype(BF16)
            k_out[rows, hd * HEAD_PAD + LANES:(hd + 1) * HEAD_PAD] = (kr_rot * rk).astype(BF16)
            vh = kv[:, MLA_HEADS * QK_NOPE + hd * V_HEAD:MLA_HEADS * QK_NOPE + (hd + 1) * V_HEAD]
            vt_out[hd * V_HEAD:(hd + 1) * V_HEAD, rows] = vh.T.astype(BF16)


def _mla_prep(x2, norm_g, w_lat, layer, qag, kvag, w_uq_p, w_ukv_p, qg_pad, kg_nope, kg_tail, tabs, tm):
    m = x2.shape[0]
    row = lambda w: pl.BlockSpec((tm, w), lambda i: (i, 0))
    full = lambda a: pl.BlockSpec(a.shape, lambda i: (0,) * a.ndim)
    col = lambda h: pl.BlockSpec((h, tm), lambda i: (0, i))
    lat = pl.BlockSpec((None, D_MODEL, LAT_W), lambda i: (layer, 0, 0))
    consts = (norm_g, w_lat, qag, kvag, w_uq_p, w_ukv_p, qg_pad, kg_nope, kg_tail)
    return pl.pallas_call(
        functools.partial(_mla_prep_kernel, n_split=2),
        grid=(m // tm,),
        in_specs=[row(D_MODEL)] + [lat if a is w_lat else full(a) for a in consts] + [row(LANES)] * 3,
        out_specs=[col(MLA_HEADS * HEAD_PAD), row(MLA_HEADS * HEAD_PAD), col(MLA_HEADS * V_HEAD)],
        out_shape=[jax.ShapeDtypeStruct((MLA_HEADS * HEAD_PAD, m), BF16),
                   jax.ShapeDtypeStruct((m, MLA_HEADS * HEAD_PAD), BF16),
                   jax.ShapeDtypeStruct((MLA_HEADS * V_HEAD, m), BF16)],
        compiler_params=_params("parallel"),
        name="mla_prep",
    )(x2, *consts, *tabs)


def _attn_kernel(qt_ref, k_ref, vt_ref, z_ref, o_ref, acc_ref, sa_ref, sb_ref, *, t, heads):
    qi = pl.program_id(2)
    acc_ref[...] = jnp.zeros(acc_ref.shape, F32)

    def scores(kj, s_ref):
        start = pl.multiple_of(kj * t, t)
        for g in range(heads):
            s_ref[g] = _dot(k_ref[pl.ds(start, t), g * HEAD_PAD:(g + 1) * HEAD_PAD],
                            qt_ref[g * HEAD_PAD:(g + 1) * HEAD_PAD, :])

    def consume(kj, s_ref, stats, masked):
        start = pl.multiple_of(kj * t, t)
        out = []
        for g in range(heads):
            m_prev, l_prev = stats[2 * g], stats[2 * g + 1]
            st = s_ref[g]
            if masked:
                kpos = kj * t + lax.broadcasted_iota(jnp.int32, (t, t), 0)
                qpos = qi * t + lax.broadcasted_iota(jnp.int32, (t, t), 1)
                st = jnp.where(kpos <= qpos, st, -jnp.inf)
            m_new = jnp.maximum(m_prev, jnp.max(st, axis=0, keepdims=True))
            alpha = jnp.exp2(m_prev - m_new)
            pt = jnp.exp2(st - m_new).astype(BF16)
            lhs = jnp.concatenate([vt_ref[g * V_HEAD:(g + 1) * V_HEAD, pl.ds(start, t)],
                                   jnp.ones((ONES_ROWS, t), BF16)], axis=0)
            pv = _dot(lhs, pt)
            l_new = alpha * l_prev + pv[V_HEAD:V_HEAD + 1, :]
            acc_ref[g] = alpha * acc_ref[g] + pv[:V_HEAD, :]
            out += [m_new, l_new]
        return tuple(out)

    def pair(i, stats):
        scores(2 * i + 1, sb_ref)
        stats = consume(2 * i, sa_ref, stats, False)
        scores(2 * i + 2, sa_ref)
        return consume(2 * i + 1, sb_ref, stats, False)

    init = (jnp.full((1, t), -jnp.inf, F32), jnp.zeros((1, t), F32)) * heads
    scores(0, sa_ref)
    stats = lax.fori_loop(0, qi // 2, pair, init)

    def tail_odd(stats):
        scores(qi, sb_ref)
        return consume(qi, sb_ref, consume(qi - 1, sa_ref, stats, False), True)

    stats = lax.cond(qi % 2 == 1, tail_odd, lambda stats: consume(qi, sa_ref, stats, True), stats)
    for g in range(heads):
        o = (acc_ref[g] / stats[2 * g + 1]).T
        o_ref[:, g * V_HEAD:(g + 1) * V_HEAD] = (o * z_ref[:, g * V_HEAD:(g + 1) * V_HEAD].astype(F32)).astype(BF16)


def _attention(qt, k, vt, proj, batch, seq, t, heads):
    m = k.shape[0]
    tq = t
    nq = seq // tq
    z_col = COL_MZ * BRANCH_W // (heads * V_HEAD)
    return pl.pallas_call(
        functools.partial(_attn_kernel, t=t, heads=heads),
        grid=(batch, MLA_HEADS // heads, nq),
        in_specs=[
            pl.BlockSpec((heads * HEAD_PAD, tq), lambda b, h, i: (h, b * nq + i)),
            pl.BlockSpec((seq, heads * HEAD_PAD), lambda b, h, i: (b, h)),
            pl.BlockSpec((heads * V_HEAD, seq), lambda b, h, i: (h, b)),
            pl.BlockSpec((tq, heads * V_HEAD), lambda b, h, i: (b * nq + i, z_col + h)),
        ],
        out_specs=pl.BlockSpec((tq, heads * V_HEAD), lambda b, h, i: (b * nq + i, h)),
        out_shape=jax.ShapeDtypeStruct((m, MLA_HEADS * V_HEAD), BF16),
        scratch_shapes=[pltpu.VMEM((heads, V_HEAD, t), F32),
                        pltpu.VMEM((heads, t, t), F32), pltpu.VMEM((heads, t, t), F32)],
        compiler_params=_params("parallel", "parallel", "arbitrary"),
        name="mla_attention",
    )(qt, k, vt, proj)


def _mem_kv_kernel(mem_ref, g_ref, w_ref, kg_ref, k_out, v_out):
    mm = mem_ref[...]
    kvm = _dot((mm * _rms(mm, D_MODEL) * g_ref[...]).astype(BF16), w_ref[...])
    for hd in range(XATTN_HEADS):
        kh = kvm[:, hd * XATTN_HEAD_DIM:(hd + 1) * XATTN_HEAD_DIM]
        k_out[:, hd * XATTN_HEAD_DIM:(hd + 1) * XATTN_HEAD_DIM] = (
            kh * _rms(kh, XATTN_HEAD_DIM) * kg_ref[...]).astype(BF16)
    v_out[...] = kvm[:, BRANCH_W:].astype(BF16)


def _mem_kv(mem2, mem_g, w_mem, kg):
    m = mem2.shape[0]
    out = jax.ShapeDtypeStruct((m, BRANCH_W), BF16)
    return pl.pallas_call(
        _mem_kv_kernel,
        grid=(m // MEM_LEN,),
        in_specs=[
            pl.BlockSpec((MEM_LEN, D_MODEL), lambda b: (b, 0)),
            pl.BlockSpec((1, D_MODEL), lambda b: (0, 0)),
            pl.BlockSpec((D_MODEL, 2 * BRANCH_W), lambda b: (0, 0)),
            pl.BlockSpec((1, XATTN_HEAD_DIM), lambda b: (0, 0)),
        ],
        out_specs=[pl.BlockSpec((MEM_LEN, BRANCH_W), lambda b: (b, 0))] * 2,
        out_shape=[out, out],
        compiler_params=_params("parallel"),
        name="mem_kv",
    )(mem2, mem_g, w_mem, kg)


def _mixers_kernel(pv_ref, cb_ref, cc_ref, cx_ref, xq_ref, pz_ref, cz_ref, xz_ref,
                   pvh_ref, cch_ref, cxh_ref, pw_ref, ps_ref, cw_ref, xqg_ref, km_ref, vm_ref,
                   ypool_ref, yconv_ref, ymem_ref, *, tm, tiles_per_seq):
    t0 = (pl.program_id(0) % tiles_per_seq) * tm
    has_prev = t0 > 0
    row = lax.broadcasted_iota(jnp.int32, (tm, 1), 0)

    r_i = lax.broadcasted_iota(jnp.int32, (tm, tm), 0)
    c_i = lax.broadcasted_iota(jnp.int32, (tm, tm), 1)
    rh_i = lax.broadcasted_iota(jnp.int32, (tm, HALO), 0)
    ch_i = lax.broadcasted_iota(jnp.int32, (tm, HALO), 1)
    pos = (t0 + row).astype(F32)
    for g, win in enumerate(POOL_WINDOWS):
        sl = slice(g * POOL_GW, (g + 1) * POOL_GW)
        vg = pv_ref[:, sl]
        band = ((c_i <= r_i) & (r_i - c_i < win)).astype(BF16)
        band_h = ((rh_i + (HALO - ch_i) < win) & has_prev).astype(BF16)
        wsum = _dot(band, vg) + _dot(band_h, pvh_ref[:, sl])
        cnt = jnp.minimum(pos + 1.0, float(win))
        mixed = wsum / cnt - vg.astype(F32)
        out = _dot(mixed.astype(BF16), pw_ref[g])
        ypool_ref[:, sl] = (out * ps_ref[:, sl] * pz_ref[:, sl].astype(F32)).astype(BF16)

    u = cc_ref[...].astype(F32) * cx_ref[...].astype(F32)
    uh = cch_ref[...].astype(F32) * cxh_ref[...].astype(F32)
    uh = jnp.where(has_prev, uh, 0.0)
    u1 = jnp.where(row == 0, uh[HALO - 1:HALO, :], pltpu.roll(u, 1, 0))
    u2 = jnp.where(row == 0, uh[HALO - 2:HALO - 1, :],
                   jnp.where(row == 1, uh[HALO - 1:HALO, :], pltpu.roll(u, 2, 0)))
    y = cw_ref[0:1, :] * u2 + cw_ref[1:2, :] * u1 + cw_ref[2:3, :] * u
    yconv_ref[...] = (cb_ref[...].astype(F32) * y * cz_ref[...].astype(F32)).astype(BF16)

    scale = XATTN_HEAD_DIM ** -0.5
    for hd in range(XATTN_HEADS):
        sl = slice(hd * XATTN_HEAD_DIM, (hd + 1) * XATTN_HEAD_DIM)
        qh = xq_ref[:, sl].astype(F32)
        qn = (qh * (_rms(qh, XATTN_HEAD_DIM) * scale) * xqg_ref[...]).astype(BF16)
        s = _dot_nt(qn, km_ref[:, sl])
        p = jnp.exp(s - jnp.max(s, axis=-1, keepdims=True))
        o = _dot(p.astype(BF16), vm_ref[:, sl]) / jnp.sum(p, axis=-1, keepdims=True)
        ymem_ref[:, sl] = (o * xz_ref[:, sl].astype(F32)).astype(BF16)


def _mixers(proj, pool_w, pool_scale, conv_w, xq_g, k_mem, v_mem, seq, tm):
    m = proj.shape[0]
    tiles_per_seq = seq // tm
    col = lambda c: pl.BlockSpec((tm, BRANCH_W), lambda i: (i, c))
    halo = lambda c: pl.BlockSpec((HALO, BRANCH_W), lambda i: (jnp.maximum(i * (tm // HALO) - 1, 0), c))
    full = lambda a: pl.BlockSpec(a.shape, lambda i: (0,) * a.ndim)
    memb = pl.BlockSpec((MEM_LEN, BRANCH_W), lambda i: (i // tiles_per_seq, 0))
    out = jax.ShapeDtypeStruct((m, BRANCH_W), BF16)
    return pl.pallas_call(
        functools.partial(_mixers_kernel, tm=tm, tiles_per_seq=tiles_per_seq),
        grid=(m // tm,),
        in_specs=[col(COL_PV), col(COL_CB), col(COL_CC), col(COL_CX), col(COL_XQ), col(COL_PZ), col(COL_CZ),
                  col(COL_XZ), halo(COL_PV), halo(COL_CC), halo(COL_CX),
                  full(pool_w), full(pool_scale), full(conv_w), full(xq_g), memb, memb],
        out_specs=[pl.BlockSpec((tm, BRANCH_W), lambda i: (i, 0))] * 3,
        out_shape=[out, out, out],
        compiler_params=_params("parallel"),
        name="mixers",
    )(*([proj] * 11), pool_w, pool_scale, conv_w, xq_g, k_mem, v_mem)


def _merge_kernel(y0_ref, y1_ref, y2_ref, y3_ref, g0_ref, g1_ref, g2_ref, g3_ref, wb_ref, o_ref):
    acc = None
    for b, (y_ref, g_ref) in enumerate(((y0_ref, g0_ref), (y1_ref, g1_ref), (y2_ref, g2_ref), (y3_ref, g3_ref))):
        term = g_ref[...].astype(F32) * _dot(y_ref[...], wb_ref[b])
        acc = term if acc is None else acc + term
    o_ref[...] = acc.astype(BF16)


def _merge(ys, proj, w_branch, tm, tn):
    m = proj.shape[0]
    gate0 = COL_GATE * BRANCH_W // tn
    gate = lambda b: pl.BlockSpec((tm, tn), lambda i, j: (i, gate0 + b * (D_MODEL // tn) + j))
    return pl.pallas_call(
        _merge_kernel,
        grid=(m // tm, D_MODEL // tn),
        in_specs=[pl.BlockSpec((tm, BRANCH_W), lambda i, j: (i, 0))] * N_BRANCH
        + [gate(b) for b in range(N_BRANCH)]
        + [pl.BlockSpec((N_BRANCH, BRANCH_W, tn), lambda i, j: (0, 0, j))],
        out_specs=pl.BlockSpec((tm, tn), lambda i, j: (i, j)),
        out_shape=jax.ShapeDtypeStruct((m, D_MODEL), BF16),
        compiler_params=_params("parallel", "arbitrary"),
        name="gated_merge",
    )(*ys, *([proj] * N_BRANCH), w_branch)


def _out_proj_kernel(x_ref, a_ref, w_ref, o_ref):
    o_ref[...] = x_ref[...] + _dot(a_ref[...], w_ref[...])


def _out_proj(x2, merged, w_out, tm, tn):
    m = x2.shape[0]
    return pl.pallas_call(
        _out_proj_kernel,
        grid=(m // tm, D_MODEL // tn),
        in_specs=[
            pl.BlockSpec((tm, tn), lambda i, j: (i, j)),
            pl.BlockSpec((tm, D_MODEL), lambda i, j: (i, 0)),
            pl.BlockSpec((D_MODEL, tn), lambda i, j: (0, j)),
        ],
        out_specs=pl.BlockSpec((tm, tn), lambda i, j: (i, j)),
        out_shape=jax.ShapeDtypeStruct((m, D_MODEL), F32),
        compiler_params=_params("parallel", "arbitrary"),
        name="out_proj",
    )(x2, merged, w_out)


MLA_COLS = Q_LORA + KV_LORA + QK_ROPE
SHIFTED_COL0 = 2 * BRANCH_W + MLA_COLS
LANE_SHIFT = SHIFTED_COL0 % LANES
SHIFTED_ORDER = (COL_MZ, COL_CB, COL_CC, COL_CX, COL_CZ, COL_XQ, COL_XZ) + tuple(
    COL_GATE + g for g in range(N_BRANCH * D_MODEL // BRANCH_W))


def _w_relayout_kernel(mb_ref, eb_ref, main_ref, extra_ref, o_ref):
    del mb_ref, eb_ref
    j = pl.program_id(2)
    aligned = (j == COL_PV) | (j == COL_PZ)

    @pl.when(aligned)
    def _():
        o_ref[...] = main_ref[...]

    @pl.when(jnp.logical_not(aligned))
    def _():
        cat = jnp.concatenate([main_ref[...].astype(F32), extra_ref[...].astype(F32)], axis=1)
        o_ref[...] = cat[:, LANE_SHIFT:LANE_SHIFT + BRANCH_W].astype(BF16)


def _w_relayout(w_in, tr):
    depth = w_in.shape[0]
    n_tiles = N_MAIN // BRANCH_W
    base = (SHIFTED_COL0 - LANE_SHIFT) // BRANCH_W
    main_blk, extra_blk = [0] * n_tiles, [0] * n_tiles
    main_blk[COL_PV], main_blk[COL_PZ] = 0, 1
    for r, slot in enumerate(SHIFTED_ORDER):
        main_blk[slot] = base + r
        extra_blk[slot] = (base + r + 1) * (BRANCH_W // LANES)
    grid_spec = pltpu.PrefetchScalarGridSpec(
        num_scalar_prefetch=2,
        grid=(depth, D_MODEL // tr, n_tiles),
        in_specs=[
            pl.BlockSpec((None, tr, BRANCH_W), lambda l, r, j, mb, eb: (l, r, mb[j])),
            pl.BlockSpec((None, tr, LANES), lambda l, r, j, mb, eb: (l, r, eb[j])),
        ],
        out_specs=pl.BlockSpec((None, tr, BRANCH_W), lambda l, r, j, mb, eb: (l, r, j)),
    )
    return pl.pallas_call(
        _w_relayout_kernel,
        grid_spec=grid_spec,
        out_shape=jax.ShapeDtypeStruct((depth, D_MODEL, N_MAIN), BF16),
        compiler_params=_params("parallel", "parallel", "arbitrary"),
        name="w_in_relayout",
    )(jnp.asarray(main_blk, jnp.int32), jnp.asarray(extra_blk, jnp.int32), w_in, w_in)


def _w_lat_relayout_kernel(qkv_ref, kr_ref, o_ref):
    o_ref[:, :Q_LORA + KV_LORA] = qkv_ref[...]
    lane = lax.broadcasted_iota(jnp.int32, kr_ref.shape, 1)
    o_ref[:, Q_LORA + KV_LORA:] = jnp.where(lane < QK_ROPE, kr_ref[...].astype(F32), 0.0).astype(BF16)


def _w_lat_relayout(w_in, tr):
    depth = w_in.shape[0]
    lat0 = 2 * BRANCH_W
    assert lat0 % (Q_LORA + KV_LORA) == 0 and (lat0 + Q_LORA + KV_LORA) % LANES == 0
    return pl.pallas_call(
        _w_lat_relayout_kernel,
        grid=(depth, D_MODEL // tr),
        in_specs=[
            pl.BlockSpec((None, tr, Q_LORA + KV_LORA), lambda l, r: (l, r, lat0 // (Q_LORA + KV_LORA))),
            pl.BlockSpec((None, tr, LANES), lambda l, r: (l, r, (lat0 + Q_LORA + KV_LORA) // LANES)),
        ],
        out_specs=pl.BlockSpec((None, tr, LAT_W), lambda l, r: (l, r, 0)),
        out_shape=jax.ShapeDtypeStruct((depth, D_MODEL, LAT_W), BF16),
        compiler_params=_params("parallel", "parallel"),
        name="w_lat_relayout",
    )(w_in, w_in)


def _layer_weights(l, norm_g, gate_b, pool_w, pool_scale, q_a_norm_g, kv_a_norm_g, w_uq, w_ukv,
                   mla_q_norm_g, mla_k_norm_g, conv_w, mem_norm_g, w_mem_kv, xattn_q_norm_g, xattn_k_norm_g,
                   w_branch, w_out):
    b_main = jnp.concatenate([jnp.zeros(((N_PLAIN + N_SILU) * BRANCH_W,), F32), gate_b[l]])[None, :]
    w_uq_p = jnp.pad(w_uq[l].reshape(Q_LORA, MLA_HEADS, QK_HEAD), ((0, 0), (0, 0), (0, HEAD_PAD - QK_HEAD)))
    w_uq_p = w_uq_p.reshape(Q_LORA, MLA_HEADS * HEAD_PAD).astype(BF16)
    w_ukv3 = w_ukv[l].reshape(KV_LORA, MLA_HEADS, QK_NOPE + V_HEAD)
    w_ukv_p = jnp.concatenate([w_ukv3[:, :, :QK_NOPE].reshape(KV_LORA, MLA_HEADS * QK_NOPE),
                               w_ukv3[:, :, QK_NOPE:].reshape(KV_LORA, MLA_HEADS * V_HEAD)], axis=1).astype(BF16)
    qg_pad = jnp.pad(mla_q_norm_g[l], (0, HEAD_PAD - QK_HEAD))[None, :]
    kg_nope = mla_k_norm_g[l][None, :QK_NOPE]
    kg_tail = jnp.pad(mla_k_norm_g[l][QK_NOPE:], (0, LANES - QK_ROPE))[None, :]
    return dict(
        norm_g=norm_g[l][None, :], b_main=b_main,
        qag=q_a_norm_g[l][None, :], kvag=kv_a_norm_g[l][None, :], w_uq_p=w_uq_p, w_ukv_p=w_ukv_p,
        qg_pad=qg_pad, kg_nope=kg_nope, kg_tail=kg_tail,
        pool_w=pool_w[l].astype(BF16), pool_scale=pool_scale[l][None, :], conv_w=conv_w[l],
        mem_g=mem_norm_g[l][None, :], w_mem=w_mem_kv[l].astype(BF16),
        xq_g=xattn_q_norm_g[l][None, :], xk_g=xattn_k_norm_g[l][None, :],
        w_branch=w_branch[l].astype(BF16), w_out=w_out[l].astype(BF16))


def _tile(n, want):
    t = min(n, want)
    assert n % t == 0, (n, want)
    return t


def kernel(x, mem, positions, norm_g, w_in, gate_b, pool_w, pool_scale, q_a_norm_g, kv_a_norm_g, w_uq, w_ukv, mla_q_norm_g, mla_k_norm_g, conv_w, mem_norm_g, w_mem_kv, xattn_q_norm_g, xattn_k_norm_g, w_branch, w_out):
    batch, seq, _ = x.shape
    m = batch * seq
    depth = w_in.shape[0]
    assert mem.shape[1] == MEM_LEN and seq % LANES == 0
    x2 = x.reshape(m, D_MODEL)
    mem2 = mem.reshape(batch * MEM_LEN, D_MODEL)
    t_big = _tile(seq, 1024)
    t_mid = _tile(seq, 512)
    tabs = _rope_tables(positions, t_big)
    w_in_bf16 = w_in.astype(BF16)
    w_main_all = _w_relayout(w_in_bf16, 1024)
    w_lat_all = _w_lat_relayout(w_in_bf16, 1024)
    for l in range(depth):
        p = _layer_weights(l, norm_g, gate_b, pool_w, pool_scale, q_a_norm_g, kv_a_norm_g, w_uq, w_ukv,
                           mla_q_norm_g, mla_k_norm_g, conv_w, mem_norm_g, w_mem_kv, xattn_q_norm_g,
                           xattn_k_norm_g, w_branch, w_out)
        proj = _in_proj(x2, p["norm_g"], w_main_all, l, p["b_main"], t_big, BRANCH_W)
        qt, k, vt = _mla_prep(x2, p["norm_g"], w_lat_all, l, p["qag"], p["kvag"], p["w_uq_p"], p["w_ukv_p"],
                              p["qg_pad"], p["kg_nope"], p["kg_tail"], tabs, t_mid)
        y_mla = _attention(qt, k, vt, proj, batch, seq, t_mid, ATTN_HEADS_PER_STEP)
        k_mem, v_mem = _mem_kv(mem2, p["mem_g"], p["w_mem"], p["xk_g"])
        y_pool, y_conv, y_mem = _mixers(proj, p["pool_w"], p["pool_scale"], p["conv_w"], p["xq_g"], k_mem, v_mem,
                                        seq, t_mid)
        merged = _merge((y_pool, y_mla, y_conv, y_mem), proj, p["w_branch"], t_big, 512)
        x2 = _out_proj(x2, merged, p["w_out"], t_big, 1024)
    return x2.reshape(batch, seq, D_MODEL)
```
